```python
import jax, jax.numpy as jnp
from jax import lax
import numpy as np

D_MODEL = 1024
BATCH = 8
SEQ = 2048
DEPTH = 1
DEC_BATCH = 128
DEC_SEQ = 1
PAST_LEN = 16384
PAGE_SIZE = 128

MIX = D_MODEL
RET_WIDTH = MIX // 2
RET_HEADS = 4
RET_DK = RET_WIDTH // RET_HEADS
RET_DV = RET_WIDTH // RET_HEADS
RET_CHUNK = 128
CONV_WIDTH = MIX - RET_WIDTH
CONV_K = 3
IN_COLS = 4 * RET_WIDTH + 3 * CONV_WIDTH
ROPE_BASE = 10000.0
PEER_HEADS = 8
PEER_NKEYS = 128
PEER_NEXPERTS = PEER_NKEYS * PEER_NKEYS
PEER_DKEY = 256
PEER_HALF = PEER_DKEY // 2
PEER_TOPK = 16
PEER_BLOCK = 128
N_MOD = 6
EPS = 1e-6

kernel_name = "hybrid_retention_shortconv_peer_step"


def rms_norm(x, g):
    xf = x.astype(jnp.float32)
    y = xf * lax.rsqrt(jnp.mean(xf * xf, axis=-1, keepdims=True) + EPS)
    return (y * g.astype(jnp.float32)).astype(x.dtype)


def modulate(x, shift, scale):
    return x * (1 + scale[:, None, :]) + shift[:, None, :]


def rotary(x, pos):
    half = x.shape[-1] // 2
    freqs = ROPE_BASE ** (-jnp.arange(half, dtype=jnp.float32) / half)
    ang = pos.astype(jnp.float32)[:, None] * freqs[None, :]
    cos = jnp.cos(ang)[None, :, None, :]
    sin = jnp.sin(ang)[None, :, None, :]
    x1, x2 = x[..., :half], x[..., half:]
    return jnp.concatenate([x1 * cos - x2 * sin, x1 * sin + x2 * cos], axis=-1)


def retention_log_decay():
    return jnp.log(1.0 - 2.0 ** (-5.0 - jnp.arange(RET_HEADS, dtype=jnp.float32)))


def retention_chunk(q, k, v, S0):
    L = q.shape[1]
    lg = retention_log_decay()
    idx = jnp.arange(L, dtype=jnp.float32)
    rel = idx[:, None] - idx[None, :]
    decay = jnp.where(rel[None] >= 0, jnp.exp(lg[:, None, None] * jnp.maximum(rel, 0.0)[None]), 0.0)
    scores = jnp.einsum('blhd,bmhd->bhlm', q, k) * decay[None]
    inner = jnp.einsum('bhlm,bmhe->blhe', scores, v)
    cross_decay = jnp.exp(lg[None, :] * (idx[:, None] + 1.0))
    cross = jnp.einsum('blhd,bhde->blhe', q, S0) * cross_decay[None, :, :, None]
    k_decay = jnp.exp(lg[None, :] * (L - 1.0 - idx[:, None]))
    S_new = jnp.exp(lg * L)[None, :, None, None] * S0 + jnp.einsum(
        'blhd,blhe->bhde', k * k_decay[None, :, :, None], v)
    return inner + cross, S_new


def retention_scan(q, k, v, S0):
    B, S, H, _ = q.shape
    C = RET_CHUNK if S % RET_CHUNK == 0 else S
    n = S // C

    def to_chunks(a):
        return a.reshape(B, n, C, H, a.shape[-1]).transpose(1, 0, 2, 3, 4)

    def step(Sc, blk):
        qc, kc, vc = blk
        o, Sn = retention_chunk(qc, kc, vc, Sc)
        return Sn, o

    S_fin, o = lax.scan(step, S0, (to_chunks(q), to_chunks(k), to_chunks(v)))
    o = o.transpose(1, 0, 2, 3, 4).reshape(B, S, H, v.shape[-1])
    return o, S_fin


def retention_group(q, k, v, g, S0, pos):
    B, S, _ = q.shape
    qf = rotary(q.astype(jnp.float32).reshape(B, S, RET_HEADS, RET_DK), pos)
    kf = rotary(k.astype(jnp.float32).reshape(B, S, RET_HEADS, RET_DK), pos) * (RET_DK ** -0.5)
    vf = v.astype(jnp.float32).reshape(B, S, RET_HEADS, RET_DV)
    o, S_new = retention_scan(qf, kf, vf, S0.astype(jnp.float32))
    mu = jnp.mean(o, axis=-1, keepdims=True)
    var = jnp.mean(jnp.square(o - mu), axis=-1, keepdims=True)
    o = ((o - mu) * lax.rsqrt(var + EPS)).reshape(B, S, RET_WIDTH)
    out = (jax.nn.silu(g.astype(jnp.float32)) * o).astype(q.dtype)
    return out, S_new


def short_conv_group(bg, cg, xv, conv_w, conv_b, buf):
    S = xv.shape[1]
    u = cg * xv
    ext = jnp.concatenate([buf.astype(u.dtype), u], axis=1)
    y = conv_b[None, None, :]
    for j in range(CONV_K):
        y = y + ext[:, j:j + S, :] * conv_w[j][None, None, :]
    return bg * y, ext[:, -(CONV_K - 1):, :]


def peer_ffn(h, w_q, sub_keys, u_tab, v_tab):
    B, S, D = h.shape
    t = h.reshape(B * S, D)
    T = t.shape[0]
    n_blocks = -(-T // PEER_BLOCK)
    pad = n_blocks * PEER_BLOCK - T
    t = jnp.pad(t, ((0, pad), (0, 0))).reshape(n_blocks, PEER_BLOCK, D)
    keys_f = sub_keys.astype(jnp.float32)

    def block(xb):
        P = xb.shape[0]
        q = jnp.einsum('pd,dk->pk', xb, w_q).astype(jnp.float32).reshape(P, PEER_HEADS, 2, PEER_HALF)
        s = jnp.einsum('phcd,hcnd->phcn', q, keys_f)
        s1, i1 = lax.top_k(s[:, :, 0], PEER_TOPK)
        s2, i2 = lax.top_k(s[:, :, 1], PEER_TOPK)
        cand = (s1[..., :, None] + s2[..., None, :]).reshape(P, PEER_HEADS, PEER_TOPK * PEER_TOPK)
        cidx = (i1[..., :, None] * PEER_NKEYS + i2[..., None, :]).reshape(P, PEER_HEADS, PEER_TOPK * PEER_TOPK)
        top_s, top_pos = lax.top_k(cand, PEER_TOPK)
        eidx = jnp.take_along_axis(cidx, top_pos, axis=-1).reshape(P, PEER_HEADS * PEER_TOPK)
        gate = jax.nn.softmax(top_s, axis=-1).reshape(P, PEER_HEADS * PEER_TOPK)
        u = u_tab[eidx]
        a = jax.nn.gelu(jnp.einsum('pd,ped->pe', xb, u))
        coef = gate.astype(xb.dtype) * a
        return jnp.einsum('pe,ped->pd', coef, v_tab[eidx])

    out = lax.map(block, t)
    return out.reshape(n_blocks * PEER_BLOCK, D)[:T].reshape(B, S, D)


def layer(x, c, pos, S0, buf, w_ada, b_ada, g_pre1, g_post1, g_pre2, g_post2,
          w_in, w_out, conv_w, conv_b, w_q, sub_keys, u_tab, v_tab):
    mod = jnp.einsum('bd,de->be', jax.nn.silu(c), w_ada) + b_ada[None, :]
    sh1, sc1, gt1, sh2, sc2, gt2 = jnp.split(mod, N_MOD, axis=-1)
    h = modulate(rms_norm(x, g_pre1), sh1, sc1)
    proj = jnp.einsum('bsd,dc->bsc', h, w_in)
    cuts = [RET_WIDTH, 2 * RET_WIDTH, 3 * RET_WIDTH, 4 * RET_WIDTH,
            4 * RET_WIDTH + CONV_WIDTH, 4 * RET_WIDTH + 2 * CONV_WIDTH]
    q, k, v, g, bg, cg, xv = jnp.split(proj, cuts, axis=-1)
    r_out, S_new = retention_group(q, k, v, g, S0, pos)
    c_out, buf_new = short_conv_group(bg, cg, xv, conv_w, conv_b, buf)
    m = jnp.einsum('bsc,cd->bsd', jnp.concatenate([r_out, c_out], axis=-1), w_out)
    x = x + gt1[:, None, :] * rms_norm(m, g_post1)
    h = modulate(rms_norm(x, g_pre2), sh2, sc2)
    f = peer_ffn(h, w_q, sub_keys, u_tab, v_tab)
    x = x + gt2[:, None, :] * rms_norm(f, g_post2)
    return x, S_new, buf_new


def setup_inputs(seed: int = 0) -> dict:
    key = jax.random.key(seed)
    ks = jax.random.split(key, 24)
    f32 = jnp.float32
    nrm = lambda k, shape, s: jax.random.normal(k, shape, f32) * s
    D = D_MODEL
    return {
        "x_prompt": nrm(ks[0], (BATCH, SEQ, D), 1.0),
        "x_sample": nrm(ks[1], (DEC_BATCH, DEC_SEQ, D), 1.0),
        "c_prompt": nrm(ks[2], (BATCH, D), 1.0),
        "c_sample": nrm(ks[3], (DEC_BATCH, D), 1.0),
        "state_ret": nrm(ks[4], (DEPTH, DEC_BATCH, RET_HEADS, RET_DK, RET_DV), 0.5),
        "cache_conv": nrm(ks[5], (DEPTH, DEC_BATCH, CONV_K - 1, CONV_WIDTH), 1.0),
        "w_ada": nrm(ks[6], (DEPTH, D, N_MOD * D), 0.5 * D ** -0.5),
        "b_ada": nrm(ks[7], (DEPTH, N_MOD * D), 0.01),
        "g_pre1": 1.0 + nrm(ks[8], (DEPTH, D), 0.05),
        "g_post1": 1.0 + nrm(ks[9], (DEPTH, D), 0.05),
        "g_pre2": 1.0 + nrm(ks[10], (DEPTH, D), 0.05),
        "g_post2": 1.0 + nrm(ks[11], (DEPTH, D), 0.05),
        "w_in": nrm(ks[12], (DEPTH, D, IN_COLS), D ** -0.5),
        "w_out": nrm(ks[13], (DEPTH, MIX, D), MIX ** -0.5),
        "conv_w": nrm(ks[14], (DEPTH, CONV_K, CONV_WIDTH), CONV_K ** -0.5),
        "conv_b": nrm(ks[15], (DEPTH, CONV_WIDTH), 0.01),
        "w_q": nrm(ks[16], (DEPTH, D, PEER_HEADS * PEER_DKEY), D ** -0.5),
        "sub_keys": nrm(ks[17], (DEPTH, PEER_HEADS, 2, PEER_NKEYS, PEER_HALF), PEER_HALF ** -0.5),
        "u_tab": nrm(ks[18], (DEPTH, PEER_NEXPERTS, D), D ** -0.5),
        "v_tab": nrm(ks[19], (DEPTH, PEER_NEXPERTS, D), PEER_HEADS ** -0.5),
    }


def reference(x_prompt, x_sample, c_prompt, c_sample, state_ret, cache_conv,
              w_ada, b_ada, g_pre1, g_post1, g_pre2, g_post2,
              w_in, w_out, conv_w, conv_b, w_q, sub_keys, u_tab, v_tab):
    pos_p = jnp.arange(SEQ, dtype=jnp.int32)
    pos_s = PAST_LEN + jnp.arange(DEC_SEQ, dtype=jnp.int32)
    xp, xs = x_prompt, x_sample
    rp_list, cp_list, rs_list, cs_list = [], [], [], []
    for l in range(DEPTH):
        params = (w_ada[l], b_ada[l], g_pre1[l], g_post1[l], g_pre2[l], g_post2[l],
                  w_in[l], w_out[l], conv_w[l], conv_b[l], w_q[l], sub_keys[l], u_tab[l], v_tab[l])
        S0_p = jnp.zeros((BATCH, RET_HEADS, RET_DK, RET_DV), jnp.float32)
        buf_p = jnp.zeros((BATCH, CONV_K - 1, CONV_WIDTH), xp.dtype)
        xp, Sp, bp = layer(xp, c_prompt, pos_p, S0_p, buf_p, *params)
        xs, Ss, bs = layer(xs, c_sample, pos_s, state_ret[l], cache_conv[l], *params)
        rp_list.append(Sp)
        cp_list.append(bp)
        rs_list.append(Ss)
        cs_list.append(bs)
    ret_state_prompt = jnp.stack(rp_list, axis=0)
    conv_cache_prompt = jnp.stack(cp_list, axis=0)
    ret_state_sample = jnp.stack(rs_list, axis=0)
    conv_cache_sample = jnp.stack(cs_list, axis=0)
    return (xp, xs, ret_state_prompt, conv_cache_prompt, ret_state_sample, conv_cache_sample)
```

```python
import functools

import jax
import jax.numpy as jnp
import numpy as np
from jax import lax
from jax.experimental import pallas as pl
from jax.experimental.pallas import tpu as pltpu

F32 = jnp.float32
BF16 = jnp.bfloat16

LANES = 128
EPS = 1e-6
ROPE_BASE = 10000.0
RET_HEADS = 4
RET_DH = 128
RET_CHUNK = 128
CONV_K = 3
PEER_HEADS = 8
PEER_NKEYS = 128
PEER_HALF = 128
PEER_TOPK = 16
N_MOD = 6
VMEM_LIMIT = 56 * 1024 * 1024

NT = (((1,), (1,)), ((), ()))
TN = (((0,), (0,)), ((), ()))


def _cparams(sem):
    return pltpu.CompilerParams(dimension_semantics=sem, vmem_limit_bytes=VMEM_LIMIT)


def _full(shape):
    n = len(shape)
    return pl.BlockSpec(shape, lambda *_: (0,) * n)


def _rms(x, g):
    return x * lax.rsqrt(jnp.mean(x * x, axis=-1, keepdims=True) + EPS) * g


def _silu(x):
    return x / (1.0 + jnp.exp(-x))


def _gelu_tanh(x):
    c = np.sqrt(2.0 / np.pi).astype(np.float32)
    return 0.5 * x * (1.0 + jnp.tanh(c * (x + 0.044715 * (x * x * x))))


def _rotary(x, cos, sin_signed):
    return x * cos + pltpu.roll(x, RET_DH // 2, axis=1) * sin_signed


def _mod_kernel(c_ref, w_ref, b_ref, o_ref):
    a = _silu(c_ref[...]).astype(BF16)
    o_ref[...] = jnp.dot(a, w_ref[...].astype(BF16), preferred_element_type=F32) + b_ref[...]


def _modulation(c, w_ada, b_ada):
    n, d = c.shape
    cols = w_ada.shape[1]
    blk = d
    return pl.pallas_call(
        _mod_kernel,
        grid=(cols // blk,),
        in_specs=[pl.BlockSpec((n, d), lambda j: (0, 0)),
                  pl.BlockSpec((d, blk), lambda j: (0, j)),
                  pl.BlockSpec((1, blk), lambda j: (0, j))],
        out_specs=pl.BlockSpec((n, blk), lambda j: (0, j)),
        out_shape=jax.ShapeDtypeStruct((n, cols), F32),
        compiler_params=_cparams(("parallel",)),
        name="mod",
    )(c, w_ada, b_ada.reshape(1, cols))


def _mix_kernel(x_ref, mod_ref, cos_ref, sin_ref, dec_ref, cdm_ref, kdm_ref, g128_ref,
                win_ref, wout_ref, gpre1_ref, gpost1_ref, gpre2_ref, cw_ref, cb_ref,
                x1_ref, h2_ref, sfin_ref, cache_ref,
                s_scr, ext_scr, mix_scr, *, tt):
    j = pl.program_id(1)
    rw = RET_HEADS * RET_DH
    cwid = cw_ref.shape[1]

    @pl.when(j == 0)
    def _():
        s_scr[...] = jnp.zeros_like(s_scr)
        ext_scr[0:8, :] = jnp.zeros((8, cwid), F32)

    x = x_ref[0]
    mod = mod_ref[0]
    sh1, sc1, gt1, sh2, sc2, gt2 = [mod[i:i + 1] for i in range(N_MOD)]
    h = (_rms(x, gpre1_ref[...]) * (1.0 + sc1) + sh1).astype(BF16)

    def proj(lo, width):
        return jnp.dot(h, win_ref[:, lo:lo + width], preferred_element_type=F32)

    pq = proj(0, rw)
    pk = proj(rw, rw)
    pv = proj(2 * rw, rw)
    pg = proj(3 * rw, rw)

    for c in range(tt // RET_CHUNK):
        r0 = c * RET_CHUNK
        cosb = cos_ref[r0:r0 + RET_CHUNK, :]
        sinb = sin_ref[r0:r0 + RET_CHUNK, :]
        for hh in range(RET_HEADS):
            l0 = hh * RET_DH
            q = _rotary(pq[r0:r0 + RET_CHUNK, l0:l0 + RET_DH], cosb, sinb)
            k = _rotary(pk[r0:r0 + RET_CHUNK, l0:l0 + RET_DH], cosb, sinb) * (RET_DH ** -0.5)
            v = pv[r0:r0 + RET_CHUNK, l0:l0 + RET_DH]
            g = pg[r0:r0 + RET_CHUNK, l0:l0 + RET_DH]
            qb, kb, vb = q.astype(BF16), k.astype(BF16), v.astype(BF16)
            s_old = s_scr[hh]
            scores = lax.dot_general(qb, kb, NT, preferred_element_type=F32) * dec_ref[hh]
            inner = jnp.dot(scores.astype(BF16), vb, preferred_element_type=F32)
            cross = jnp.dot(qb, s_old.astype(BF16), preferred_element_type=F32) * cdm_ref[hh]
            o = inner + cross
            kd = (k * kdm_ref[hh]).astype(BF16)
            s_scr[hh] = g128_ref[hh] * s_old + lax.dot_general(kd, vb, TN, preferred_element_type=F32)
            mu = jnp.mean(o, axis=-1, keepdims=True)
            oc = o - mu
            on = oc * lax.rsqrt(jnp.mean(oc * oc, axis=-1, keepdims=True) + EPS)
            mix_scr[r0:r0 + RET_CHUNK, l0:l0 + RET_DH] = (_silu(g) * on).astype(BF16)

    pb = proj(4 * rw, cwid)
    u = proj(4 * rw + cwid, cwid) * proj(4 * rw + 2 * cwid, cwid)
    ext_scr[8:8 + tt, :] = u
    y = (cb_ref[...] + cw_ref[0:1, :] * ext_scr[6:6 + tt, :]
         + cw_ref[1:2, :] * ext_scr[7:7 + tt, :] + cw_ref[2:3, :] * u)
    mix_scr[:, rw:rw + cwid] = (pb * y).astype(BF16)
    cache_ref[0] = ext_scr[tt + 6:tt + 8, :]
    ext_scr[0:8, :] = ext_scr[tt:tt + 8, :]

    m = jnp.dot(mix_scr[...], wout_ref[...], preferred_element_type=F32)
    x1 = x + gt1 * _rms(m, gpost1_ref[...])
    x1_ref[0] = x1
    h2_ref[0] = (_rms(x1, gpre2_ref[...]) * (1.0 + sc2) + sh2).astype(BF16)

    @pl.when(j == pl.num_programs(1) - 1)
    def _():
        sfin_ref[0] = s_scr[...]


def _decay_tables():
    lg = jnp.log(1.0 - 2.0 ** (-5.0 - jnp.arange(RET_HEADS, dtype=F32)))
    L = RET_CHUNK
    idx = jnp.arange(L, dtype=F32)
    rel = idx[:, None] - idx[None, :]
    dec = jnp.where(rel[None] >= 0, jnp.exp(lg[:, None, None] * jnp.maximum(rel, 0.0)[None]), 0.0)
    ones = jnp.ones((1, 1, RET_DH), F32)
    cdm = jnp.exp(lg[:, None] * (idx[None, :] + 1.0))[:, :, None] * ones
    kdm = jnp.exp(lg[:, None] * (L - 1.0 - idx[None, :]))[:, :, None] * ones
    g128 = jnp.exp(lg * L)[:, None, None] * jnp.ones((1, L, RET_DH), F32)
    return dec, cdm, kdm, g128


def _rope_tables(pos):
    half = RET_DH // 2
    freqs = ROPE_BASE ** (-jnp.arange(half, dtype=F32) / half)
    ang = pos.astype(F32)[:, None] * freqs[None, :]
    cos, sin = jnp.cos(ang), jnp.sin(ang)
    return jnp.concatenate([cos, cos], axis=-1), jnp.concatenate([-sin, sin], axis=-1)


def _prompt_mixer(x, mod, w_in, w_out, g_pre1, g_post1, g_pre2, conv_w, conv_b, tt):
    b, s, d = x.shape
    rw = RET_HEADS * RET_DH
    cwid = conv_w.shape[1]
    cos, sin = _rope_tables(jnp.arange(s, dtype=jnp.int32))
    dec, cdm, kdm, g128 = _decay_tables()
    tab = pl.BlockSpec((RET_HEADS, RET_CHUNK, RET_DH), lambda i, j: (0, 0, 0))
    row = pl.BlockSpec((1, d), lambda i, j: (0, 0))
    return pl.pallas_call(
        functools.partial(_mix_kernel, tt=tt),
        grid=(b, s // tt),
        in_specs=[pl.BlockSpec((1, tt, d), lambda i, j: (i, j, 0)),
                  pl.BlockSpec((1, N_MOD, d), lambda i, j: (i, 0, 0)),
                  pl.BlockSpec((tt, RET_DH), lambda i, j: (j, 0)),
                  pl.BlockSpec((tt, RET_DH), lambda i, j: (j, 0)),
                  tab, tab, tab, tab,
                  pl.BlockSpec(w_in.shape, lambda i, j: (0, 0)),
                  pl.BlockSpec(w_out.shape, lambda i, j: (0, 0)),
                  row, row, row,
                  pl.BlockSpec((CONV_K, cwid), lambda i, j: (0, 0)),
                  pl.BlockSpec((1, cwid), lambda i, j: (0, 0))],
        out_specs=[pl.BlockSpec((1, tt, d), lambda i, j: (i, j, 0)),
                   pl.BlockSpec((1, tt, d), lambda i, j: (i, j, 0)),
                   pl.BlockSpec((1, RET_HEADS, RET_DH, RET_DH), lambda i, j: (i, 0, 0, 0)),
                   pl.BlockSpec((1, CONV_K - 1, cwid), lambda i, j: (i, 0, 0))],
        out_shape=[jax.ShapeDtypeStruct((b, s, d), F32),
                   jax.ShapeDtypeStruct((b, s, d), BF16),
                   jax.ShapeDtypeStruct((b, RET_HEADS, RET_DH, RET_DH), F32),
                   jax.ShapeDtypeStruct((b, CONV_K - 1, cwid), F32)],
        scratch_shapes=[pltpu.VMEM((RET_HEADS, RET_DH, RET_DH), F32),
                        pltpu.VMEM((tt + 8, cwid), F32),
                        pltpu.VMEM((tt, rw + cwid), BF16)],
        compiler_params=_cparams(("parallel", "arbitrary")),
        name="mix",
    )(x, mod, cos, sin, dec, cdm, kdm, g128, w_in, w_out,
      g_pre1.reshape(1, d), g_post1.reshape(1, d), g_pre2.reshape(1, d),
      conv_w, conv_b.reshape(1, cwid))


SMP_BLOCK = 8


def _ret_gamma(hh):
    return 1.0 - 2.0 ** (-5.0 - hh)


def _smp_proj_kernel(x_ref, mod_ref, cos_ref, sin_ref, win_ref, gpre1_ref, cw_ref, cb_ref, cache_ref,
                     q_ref, k_ref, v_ref, g_ref, cout_ref, cache_out_ref):
    d = x_ref.shape[1]
    rw = RET_HEADS * RET_DH
    cwid = cw_ref.shape[1]
    x = x_ref[...]
    sh1 = mod_ref[:, 0:d]
    sc1 = mod_ref[:, d:2 * d]
    h = (_rms(x, gpre1_ref[...]) * (1.0 + sc1) + sh1).astype(BF16)
    p = jnp.dot(h, win_ref[...], preferred_element_type=F32)
    cosb, sinb = cos_ref[...], sin_ref[...]
    for hh in range(RET_HEADS):
        l0 = hh * RET_DH
        q_ref[:, l0:l0 + RET_DH] = _rotary(p[:, l0:l0 + RET_DH], cosb, sinb)
        k_ref[:, l0:l0 + RET_DH] = _rotary(p[:, rw + l0:rw + l0 + RET_DH], cosb, sinb) * (RET_DH ** -0.5)
    v_ref[...] = p[:, 2 * rw:3 * rw]
    g_ref[...] = p[:, 3 * rw:4 * rw]
    pb = p[:, 4 * rw:4 * rw + cwid]
    u = p[:, 4 * rw + cwid:4 * rw + 2 * cwid] * p[:, 4 * rw + 2 * cwid:4 * rw + 3 * cwid]
    buf0 = cache_ref[:, 0:cwid]
    buf1 = cache_ref[:, cwid:2 * cwid]
    y = cb_ref[...] + cw_ref[0:1, :] * buf0 + cw_ref[1:2, :] * buf1 + cw_ref[2:3, :] * u
    cout_ref[...] = pb * y
    cache_out_ref[:, 0:cwid] = buf1
    cache_out_ref[:, cwid:2 * cwid] = u


def _smp_state_kernel(q_ref, k_ref, v_ref, s0_ref, o_ref, snew_ref):
    nb = SMP_BLOCK
    row = lax.broadcasted_iota(jnp.int32, (nb, RET_DH), 0)
    zpad = jnp.zeros((RET_CHUNK - nb, RET_DH), F32)
    for hh in range(RET_HEADS):
        l0 = hh * RET_DH
        gam = _ret_gamma(hh)
        qb = q_ref[:, l0:l0 + RET_DH].astype(BF16)
        kb = k_ref[:, l0:l0 + RET_DH].astype(BF16)
        vb = v_ref[:, l0:l0 + RET_DH].astype(BF16)
        s_list = [s0_ref[n, hh] for n in range(nb)]
        scat = jnp.concatenate(s_list, axis=1).astype(BF16)
        res = jnp.dot(qb, scat, preferred_element_type=F32)
        cross = jnp.zeros((nb, RET_DH), F32)
        for n in range(nb):
            cross = cross + jnp.where(row == n, res[:, n * RET_DH:(n + 1) * RET_DH], 0.0)
        sc = jnp.sum(qb.astype(F32) * kb.astype(F32), axis=-1, keepdims=True)
        o_ref[:, l0:l0 + RET_DH] = sc.astype(BF16).astype(F32) * vb.astype(F32) + cross * gam
        kpad = jnp.concatenate([kb.astype(F32), zpad], axis=0).astype(BF16)
        vf = vb.astype(F32)
        w = jnp.concatenate([jnp.where(row == n, vf, 0.0) for n in range(nb)], axis=1)
        wpad = jnp.concatenate([w, jnp.zeros((RET_CHUNK - nb, nb * RET_DH), F32)], axis=0).astype(BF16)
        kv = lax.dot_general(kpad, wpad, TN, preferred_element_type=F32)
        for n in range(nb):
            snew_ref[n, hh] = gam * s_list[n] + kv[:, n * RET_DH:(n + 1) * RET_DH]


def _smp_post_kernel(x_ref, mod_ref, o_ref, g_ref, cout_ref, wout_ref, gpost1_ref, gpre2_ref,
                     x1_ref, h2_ref):
    d = x_ref.shape[1]
    rw = RET_HEADS * RET_DH
    x = x_ref[...]
    gt1 = mod_ref[:, 2 * d:3 * d]
    sh2 = mod_ref[:, 3 * d:4 * d]
    sc2 = mod_ref[:, 4 * d:5 * d]
    m = jnp.dot(cout_ref[...].astype(BF16), wout_ref[rw:, :], preferred_element_type=F32)
    for hh in range(RET_HEADS):
        l0 = hh * RET_DH
        o = o_ref[:, l0:l0 + RET_DH]
        oc = o - jnp.mean(o, axis=-1, keepdims=True)
        on = oc * lax.rsqrt(jnp.mean(oc * oc, axis=-1, keepdims=True) + EPS)
        r = (_silu(g_ref[:, l0:l0 + RET_DH]) * on).astype(BF16)
        m = m + jnp.dot(r, wout_ref[l0:l0 + RET_DH, :], preferred_element_type=F32)
    x1 = x + gt1 * _rms(m, gpost1_ref[...])
    x1_ref[...] = x1
    h2_ref[...] = (_rms(x1, gpre2_ref[...]) * (1.0 + sc2) + sh2).astype(BF16)


def _sample_mixer(x, mod, state, cache, pos, w_in, w_out, g_pre1, g_post1, g_pre2, conv_w, conv_b):
    n, d = x.shape
    rw = RET_HEADS * RET_DH
    cwid = conv_w.shape[1]
    cos, sin = _rope_tables(pos)
    f = jax.ShapeDtypeStruct
    q, k, v, g, cout, cache_out = pl.pallas_call(
        _smp_proj_kernel,
        out_shape=[f((n, rw), F32)] * 4 + [f((n, cwid), F32), f((n, (CONV_K - 1) * cwid), F32)],
        compiler_params=pltpu.CompilerParams(vmem_limit_bytes=VMEM_LIMIT),
        name="smp_proj",
    )(x, mod, cos, sin, w_in, g_pre1.reshape(1, d), conv_w, conv_b.reshape(1, cwid),
      cache.reshape(n, (CONV_K - 1) * cwid))
    nb = SMP_BLOCK
    vec = pl.BlockSpec((nb, rw), lambda i: (i, 0))
    st = pl.BlockSpec((nb, RET_HEADS, RET_DH, RET_DH), lambda i: (i, 0, 0, 0))
    o, snew = pl.pallas_call(
        _smp_state_kernel,
        grid=(n // nb,),
        in_specs=[vec, vec, vec, st],
        out_specs=[vec, st],
        out_shape=[f((n, rw), F32), f(state.shape, F32)],
        compiler_params=_cparams(("parallel",)),
        name="smp_state",
    )(q, k, v, state)
    x1, h2 = pl.pallas_call(
        _smp_post_kernel,
        out_shape=[f((n, d), F32), f((n, d), BF16)],
        compiler_params=pltpu.CompilerParams(vmem_limit_bytes=VMEM_LIMIT),
        name="smp_post",
    )(x, mod, o, g, cout, w_out, g_post1.reshape(1, d), g_pre2.reshape(1, d))
    return x1, h2, snew, cache_out.reshape(n, CONV_K - 1, cwid)


_CAND = [(a, b) for a in range(PEER_TOPK) for b in range(PEER_TOPK) if (a + 1) * (b + 1) <= PEER_TOPK]


def _sel_kernel(h2_ref, wq_ref, keys_ref, e1_ref, cnt_ref, e2_ref, r2_ref,
                s1_scr, s2_scr, a_scr, b_scr, *, tt):
    ntc = tt // LANES
    neg = -jnp.inf
    q = jnp.dot(h2_ref[...], wq_ref[...], preferred_element_type=F32).astype(BF16)
    for h in range(PEER_HEADS):
        for c, scr in ((0, s1_scr), (1, s2_scr)):
            col = (h * 2 + c) * PEER_HALF
            scr[h] = lax.dot_general(keys_ref[h, c], q[:, col:col + PEER_HALF], NT,
                                     preferred_element_type=F32)

    def token_chunk(tc, carry):
        lanes = pl.ds(pl.multiple_of(tc * LANES, LANES), LANES)
        for h in range(PEER_HEADS):
            for side, (s_scr, v_scr) in enumerate(((s1_scr, a_scr), (s2_scr, b_scr))):
                s = s_scr[h, :, lanes]
                rank = jnp.full(s.shape, float(PEER_TOPK), F32)
                for k in range(PEER_TOPK):
                    m = jnp.max(s, axis=0, keepdims=True)
                    hit = s == m
                    if side == 1:
                        rank = jnp.where(hit, float(k), rank)
                    s = jnp.where(hit, neg, s)
                    v_scr[k, h:h + 1, lanes] = m
                if side == 1:
                    r2_ref[h, :, lanes] = rank.astype(BF16)
        av = [a_scr[k, :, lanes] for k in range(PEER_TOPK)]
        bv = [b_scr[k, :, lanes] for k in range(PEER_TOPK)]
        cands = [av[a] + bv[b] for a, b in _CAND]
        cur = list(cands)
        t = None
        for r in range(PEER_TOPK):
            t = functools.reduce(jnp.maximum, cur)
            if r + 1 < PEER_TOPK:
                cur = [jnp.where(c == t, neg, c) for c in cur]
        z = None
        for c in cands:
            term = jnp.where(c >= t, jnp.exp(c - cands[0]), 0.0)
            z = term if z is None else z + term
        zi = 1.0 / z
        for h in range(PEER_HEADS):
            s1 = s1_scr[h, :, lanes]
            s2 = s2_scr[h, :, lanes]
            th = t[h:h + 1]
            cnt = jnp.zeros(s1.shape, F32)
            for k in range(PEER_TOPK):
                cnt = cnt + jnp.where(s1 + bv[k][h:h + 1] >= th, 1.0, 0.0)
            cnt_ref[h, :, lanes] = cnt.astype(BF16)
            e1_ref[h, :, lanes] = (jnp.exp(s1 - av[0][h:h + 1]) * zi[h:h + 1]).astype(BF16)
            e2_ref[h, :, lanes] = jnp.exp(s2 - bv[0][h:h + 1]).astype(BF16)
        return carry

    lax.fori_loop(0, ntc, token_chunk, 0)


def _peer_select(h2, w_q, keys, tt):
    t, d = h2.shape
    out = jax.ShapeDtypeStruct((PEER_HEADS, PEER_NKEYS, t), BF16)
    ospec = pl.BlockSpec((PEER_HEADS, PEER_NKEYS, tt), lambda i: (0, 0, i))
    return pl.pallas_call(
        functools.partial(_sel_kernel, tt=tt),
        grid=(t // tt,),
        in_specs=[pl.BlockSpec((tt, d), lambda i: (i, 0)),
                  pl.BlockSpec(w_q.shape, lambda i: (0, 0)),
                  pl.BlockSpec(keys.shape, lambda i: (0, 0, 0, 0))],
        out_specs=[ospec] * 4,
        out_shape=[out] * 4,
        scratch_shapes=[pltpu.VMEM((PEER_HEADS, PEER_NKEYS, tt), F32),
                        pltpu.VMEM((PEER_HEADS, PEER_NKEYS, tt), F32),
                        pltpu.VMEM((PEER_TOPK, PEER_HEADS, tt), F32),
                        pltpu.VMEM((PEER_TOPK, PEER_HEADS, tt), F32)],
        compiler_params=_cparams(("parallel",)),
        name="sel",
    )(h2, w_q, keys)


PEER_ECHUNK = 512
PEER_JSUB = 16


def _peer_kernel(h2_ref, u_ref, vt_ref, e1_ref, cnt_ref, e2_ref, r2_ref, x1_ref, gt2_ref, gpost2_ref,
                 y_ref, acc_ref, c_scr, *, eb):
    e = pl.program_id(1)

    @pl.when(e == 0)
    def _():
        acc_ref[...] = jnp.zeros_like(acc_ref)

    xb = h2_ref[...]
    part = None
    for c in range(eb // PEER_ECHUNK):
        c0 = c * PEER_ECHUNK
        a = lax.dot_general(u_ref[c0:c0 + PEER_ECHUNK, :], xb, NT, preferred_element_type=F32)
        for il in range(PEER_ECHUNK // PEER_NKEYS):
            i = c * (PEER_ECHUNK // PEER_NKEYS) + il
            e1r = [e1_ref[h, i:i + 1, :] for h in range(PEER_HEADS)]
            cntr = [cnt_ref[h, i:i + 1, :] for h in range(PEER_HEADS)]
            for jc in range(PEER_NKEYS // PEER_JSUB):
                j0 = jc * PEER_JSUB
                g = None
                for h in range(PEER_HEADS):
                    term = jnp.where(r2_ref[h, j0:j0 + PEER_JSUB, :] < cntr[h],
                                     e2_ref[h, j0:j0 + PEER_JSUB, :], 0.0) * e1r[h]
                    g = term if g is None else g + term
                r0 = il * PEER_NKEYS + j0
                act = _gelu_tanh(a[r0:r0 + PEER_JSUB, :])
                c_scr[c0 + r0:c0 + r0 + PEER_JSUB, :] = (act * g.astype(F32)).astype(BF16)
        d2 = jnp.dot(vt_ref[:, c0:c0 + PEER_ECHUNK], c_scr[c0:c0 + PEER_ECHUNK, :],
                     preferred_element_type=F32)
        part = d2 if part is None else part + d2
    acc_ref[...] += part

    @pl.when(e == pl.num_programs(1) - 1)
    def _():
        f = acc_ref[...].T
        gt2 = gt2_ref[0] if len(gt2_ref.shape) == 3 else gt2_ref[...]
        y_ref[...] = x1_ref[...] + gt2 * _rms(f, gpost2_ref[...])


def _peer_dense(h2, x1, gt2, ub, vt, sel, g_post2, tt, eb, seq_len):
    t, d = h2.shape
    ne = ub.shape[0]
    e1, cnt, e2, r2 = sel
    ib = eb // PEER_NKEYS
    per_e = pl.BlockSpec((PEER_HEADS, ib, tt), lambda i, e: (0, e, i))
    per_t = pl.BlockSpec((PEER_HEADS, PEER_NKEYS, tt), lambda i, e: (0, 0, i))
    if seq_len == 1:
        gspec = pl.BlockSpec((tt, d), lambda i, e: (i, 0))
    else:
        assert seq_len % tt == 0
        tiles_per_seq = seq_len // tt
        gspec = pl.BlockSpec((1, 1, d), lambda i, e: (i // tiles_per_seq, 0, 0))
        gt2 = gt2.reshape(gt2.shape[0], 1, d)
    return pl.pallas_call(
        functools.partial(_peer_kernel, eb=eb),
        grid=(t // tt, ne // eb),
        in_specs=[pl.BlockSpec((tt, d), lambda i, e: (i, 0)),
                  pl.BlockSpec((eb, d), lambda i, e: (e, 0)),
                  pl.BlockSpec((d, eb), lambda i, e: (0, e)),
                  per_e, per_e, per_t, per_t,
                  pl.BlockSpec((tt, d), lambda i, e: (i, 0)),
                  gspec,
                  pl.BlockSpec((1, d), lambda i, e: (0, 0))],
        out_specs=pl.BlockSpec((tt, d), lambda i, e: (i, 0)),
        out_shape=jax.ShapeDtypeStruct((t, d), F32),
        scratch_shapes=[pltpu.VMEM((d, tt), F32), pltpu.VMEM((eb, tt), BF16)],
        compiler_params=_cparams(("parallel", "arbitrary")),
        name="peer",
    )(h2, ub, vt, e1, cnt, e2, r2, x1, gt2, g_post2.reshape(1, d))


def _prep_kernel(u_ref, v_ref, ub_ref, vt_ref):
    ub_ref[...] = u_ref[...].astype(BF16)
    vt_ref[...] = v_ref[...].T.astype(BF16)


def _prep_tables(u_tab, v_tab, blk=512):
    ne, d = u_tab.shape
    return pl.pallas_call(
        _prep_kernel,
        grid=(ne // blk,),
        in_specs=[pl.BlockSpec((blk, d), lambda i: (i, 0)), pl.BlockSpec((blk, d), lambda i: (i, 0))],
        out_specs=[pl.BlockSpec((blk, d), lambda i: (i, 0)), pl.BlockSpec((d, blk), lambda i: (0, i))],
        out_shape=[jax.ShapeDtypeStruct((ne, d), BF16), jax.ShapeDtypeStruct((d, ne), BF16)],
        compiler_params=_cparams(("parallel",)),
        name="prep",
    )(u_tab, v_tab)


PAST_LEN = 16384
MIX_TILE = 256
SEL_TILE = 512
PEER_TILE = 512
PEER_EBLOCK = 2048


def kernel(x_prompt, x_sample, c_prompt, c_sample, state_ret, cache_conv, w_ada, b_ada, g_pre1, g_post1, g_pre2, g_post2, w_in, w_out, conv_w, conv_b, w_q, sub_keys, u_tab, v_tab):
    depth = w_in.shape[0]
    assert depth == 1, "single-layer stack"
    l = 0
    b, s, d = x_prompt.shape
    n = x_sample.shape[0]
    assert x_sample.shape[1] == 1
    mod =_modulation(jnp.concatenate([c_prompt, c_sample], axis=0), w_ada[l], b_ada[l])
    mod_p, mod_s = mod[:b], mod[b:]
    w_in_b, w_out_b = w_in[l].astype(BF16), w_out[l].astype(BF16)
    w_q_b, keys_b = w_q[l].astype(BF16), sub_keys[l].astype(BF16)
    ub, vt = _prep_tables(u_tab[l], v_tab[l])

    x1p, h2p, sfin_p, cache_p = _prompt_mixer(
        x_prompt, mod_p.reshape(b, N_MOD, d), w_in_b, w_out_b,
        g_pre1[l], g_post1[l], g_pre2[l], conv_w[l], conv_b[l], MIX_TILE)
    x1s, h2s, sfin_s, cache_s = _sample_mixer(
        x_sample.reshape(n, d), mod_s, state_ret[l], cache_conv[l],
        PAST_LEN + jnp.arange(1, dtype=jnp.int32), w_in_b, w_out_b,
        g_pre1[l], g_post1[l], g_pre2[l], conv_w[l], conv_b[l])

    h2p = h2p.reshape(b * s, d)
    sel_p = _peer_select(h2p, w_q_b, keys_b, min(SEL_TILE, s))
    yp = _peer_dense(h2p, x1p.reshape(b * s, d), mod_p[:, 5 * d:6 * d], ub, vt, sel_p, g_post2[l],
                     min(PEER_TILE, s), PEER_EBLOCK, s)
    sel_s = _peer_select(h2s, w_q_b, keys_b, n)
    ys = _peer_dense(h2s, x1s, mod_s[:, 5 * d:6 * d], ub, vt, sel_s, g_post2[l], n, PEER_EBLOCK, 1)

    return (yp.reshape(b, s, d), ys.reshape(n, 1, d), sfin_p[None], cache_p[None],
            sfin_s[None], cache_s[None])
```

```python
import functools

import jax
import jax.numpy as jnp
import numpy as np
from jax import lax
from jax.experimental import pallas as pl
from jax.experimental.pallas import tpu as pltpu

F32 = jnp.float32
BF16 = jnp.bfloat16

LANES = 128
EPS = 1e-6
ROPE_BASE = 10000.0
RET_HEADS = 4
RET_DH = 128
RET_CHUNK = 128
CONV_K = 3
PEER_HEADS = 8
PEER_NKEYS = 128
PEER_HALF = 128
PEER_TOPK = 16
N_MOD = 6
VMEM_LIMIT = 56 * 1024 * 1024

NT = (((1,), (1,)), ((), ()))
TN = (((0,), (0,)), ((), ()))


def _cparams(sem):
    return pltpu.CompilerParams(dimension_semantics=sem, vmem_limit_bytes=VMEM_LIMIT)


def _full(shape):
    n = len(shape)
    return pl.BlockSpec(shape, lambda *_: (0,) * n)


def _rms(x, g):
    return x * lax.rsqrt(jnp.mean(x * x, axis=-1, keepdims=True) + EPS) * g


def _silu(x):
    return x / (1.0 + jnp.exp(-x))


def _gelu_tanh(x):
    c = np.sqrt(2.0 / np.pi).astype(np.float32)
    return 0.5 * x * (1.0 + jnp.tanh(c * (x + 0.044715 * (x * x * x))))


def _rotary(x, cos, sin_signed):
    return x * cos + pltpu.roll(x, RET_DH // 2, axis=1) * sin_signed


def _mod_kernel(c_ref, w_ref, b_ref, o_ref):
    a = _silu(c_ref[...]).astype(BF16)
    o_ref[...] = jnp.dot(a, w_ref[...].astype(BF16), preferred_element_type=F32) + b_ref[...]


def _modulation(c, w_ada, b_ada):
    n, d = c.shape
    cols = w_ada.shape[1]
    blk = d
    return pl.pallas_call(
        _mod_kernel,
        grid=(cols // blk,),
        in_specs=[pl.BlockSpec((n, d), lambda j: (0, 0)),
                  pl.BlockSpec((d, blk), lambda j: (0, j)),
                  pl.BlockSpec((1, blk), lambda j: (0, j))],
        out_specs=pl.BlockSpec((n, blk), lambda j: (0, j)),
        out_shape=jax.ShapeDtypeStruct((n, cols), F32),
        compiler_params=_cparams(("parallel",)),
        name="mod",
    )(c, w_ada, b_ada.reshape(1, cols))


def _mix_kernel(x_ref, mod_ref, cos_ref, sin_ref, dec_ref, cdm_ref, kdm_ref, g128_ref,
                win_ref, wout_ref, gpre1_ref, gpost1_ref, gpre2_ref, cw_ref, cb_ref,
                x1_ref, h2_ref, sfin_ref, cache_ref,
                s_scr, ext_scr, mix_scr, *, tt):
    j = pl.program_id(1)
    rw = RET_HEADS * RET_DH
    cwid = cw_ref.shape[1]

    @pl.when(j == 0)
    def _():
        s_scr[...] = jnp.zeros_like(s_scr)
        ext_scr[0:8, :] = jnp.zeros((8, cwid), F32)

    x = x_ref[0]
    mod = mod_ref[0]
    sh1, sc1, gt1, sh2, sc2, gt2 = [mod[i:i + 1] for i in range(N_MOD)]
    h = (_rms(x, gpre1_ref[...]) * (1.0 + sc1) + sh1).astype(BF16)

    def proj(lo, width):
        return jnp.dot(h, win_ref[:, lo:lo + width], preferred_element_type=F32)

    pq = proj(0, rw)
    pk = proj(rw, rw)
    pv = proj(2 * rw, rw)
    pg = proj(3 * rw, rw)

    for c in range(tt // RET_CHUNK):
        r0 = c * RET_CHUNK
        cosb = cos_ref[r0:r0 + RET_CHUNK, :]
        sinb = sin_ref[r0:r0 + RET_CHUNK, :]
        for hh in range(RET_HEADS):
            l0 = hh * RET_DH
            q = _rotary(pq[r0:r0 + RET_CHUNK, l0:l0 + RET_DH], cosb, sinb)
            k = _rotary(pk[r0:r0 + RET_CHUNK, l0:l0 + RET_DH], cosb, sinb) * (RET_DH ** -0.5)
            v = pv[r0:r0 + RET_CHUNK, l0:l0 + RET_DH]
            g = pg[r0:r0 + RET_CHUNK, l0:l0 + RET_DH]
            qb, kb, vb = q.astype(BF16), k.astype(BF16), v.astype(BF16)
            s_old = s_scr[hh]
            scores = lax.dot_general(qb, kb, NT, preferred_element_type=F32) * dec_ref[hh]
            inner = jnp.dot(scores.astype(BF16), vb, preferred_element_type=F32)
            cross = jnp.dot(qb, s_old.astype(BF16), preferred_element_type=F32) * cdm_ref[hh]
            o = inner + cross
            kd = (k * kdm_ref[hh]).astype(BF16)
            s_scr[hh] = g128_ref[hh] * s_old + lax.dot_general(kd, vb, TN, preferred_element_type=F32)
            mu = jnp.mean(o, axis=-1, keepdims=True)
            oc = o - mu
            on = oc * lax.rsqrt(jnp.mean(oc * oc, axis=-1, keepdims=True) + EPS)
            mix_scr[r0:r0 + RET_CHUNK, l0:l0 + RET_DH] = (_silu(g) * on).astype(BF16)

    pb = proj(4 * rw, cwid)
    u = proj(4 * rw + cwid, cwid) * proj(4 * rw + 2 * cwid, cwid)
    ext_scr[8:8 + tt, :] = u
    y = (cb_ref[...] + cw_ref[0:1, :] * ext_scr[6:6 + tt, :]
         + cw_ref[1:2, :] * ext_scr[7:7 + tt, :] + cw_ref[2:3, :] * u)
    mix_scr[:, rw:rw + cwid] = (pb * y).astype(BF16)
    cache_ref[0] = ext_scr[tt + 6:tt + 8, :]
    ext_scr[0:8, :] = ext_scr[tt:tt + 8, :]

    m = jnp.dot(mix_scr[...], wout_ref[...], preferred_element_type=F32)
    x1 = x + gt1 * _rms(m, gpost1_ref[...])
    x1_ref[0] = x1
    h2_ref[0] = (_rms(x1, gpre2_ref[...]) * (1.0 + sc2) + sh2).astype(BF16)

    @pl.when(j == pl.num_programs(1) - 1)
    def _():
        sfin_ref[0] = s_scr[...]


def _decay_tables():
    lg = jnp.log(1.0 - 2.0 ** (-5.0 - jnp.arange(RET_HEADS, dtype=F32)))
    L = RET_CHUNK
    idx = jnp.arange(L, dtype=F32)
    rel = idx[:, None] - idx[None, :]
    dec = jnp.where(rel[None] >= 0, jnp.exp(lg[:, None, None] * jnp.maximum(rel, 0.0)[None]), 0.0)
    ones = jnp.ones((1, 1, RET_DH), F32)
    cdm = jnp.exp(lg[:, None] * (idx[None, :] + 1.0))[:, :, None] * ones
    kdm = jnp.exp(lg[:, None] * (L - 1.0 - idx[None, :]))[:, :, None] * ones
    g128 = jnp.exp(lg * L)[:, None, None] * jnp.ones((1, L, RET_DH), F32)
    return dec, cdm, kdm, g128


def _rope_tables(pos):
    half = RET_DH // 2
    freqs = ROPE_BASE ** (-jnp.arange(half, dtype=F32) / half)
    ang = pos.astype(F32)[:, None] * freqs[None, :]
    cos, sin = jnp.cos(ang), jnp.sin(ang)
    return jnp.concatenate([cos, cos], axis=-1), jnp.concatenate([-sin, sin], axis=-1)


def _prompt_mixer(x, mod, w_in, w_out, g_pre1, g_post1, g_pre2, conv_w, conv_b, tt):
    b, s, d = x.shape
    rw = RET_HEADS * RET_DH
    cwid = conv_w.shape[1]
    cos, sin = _rope_tables(jnp.arange(s, dtype=jnp.int32))
    dec, cdm, kdm, g128 = _decay_tables()
    tab = pl.BlockSpec((RET_HEADS, RET_CHUNK, RET_DH), lambda i, j: (0, 0, 0))
    row = pl.BlockSpec((1, d), lambda i, j: (0, 0))
    return pl.pallas_call(
        functools.partial(_mix_kernel, tt=tt),
        grid=(b, s // tt),
        in_specs=[pl.BlockSpec((1, tt, d), lambda i, j: (i, j, 0)),
                  pl.BlockSpec((1, N_MOD, d), lambda i, j: (i, 0, 0)),
                  pl.BlockSpec((tt, RET_DH), lambda i, j: (j, 0)),
                  pl.BlockSpec((tt, RET_DH), lambda i, j: (j, 0)),
                  tab, tab, tab, tab,
                  pl.BlockSpec(w_in.shape, lambda i, j: (0, 0)),
                  pl.BlockSpec(w_out.shape, lambda i, j: (0, 0)),
                  row, row, row,
                  pl.BlockSpec((CONV_K, cwid), lambda i, j: (0, 0)),
                  pl.BlockSpec((1, cwid), lambda i, j: (0, 0))],
        out_specs=[pl.BlockSpec((1, tt, d), lambda i, j: (i, j, 0)),
                   pl.BlockSpec((1, tt, d), lambda i, j: (i, j, 0)),
                   pl.BlockSpec((1, RET_HEADS, RET_DH, RET_DH), lambda i, j: (i, 0, 0, 0)),
                   pl.BlockSpec((1, CONV_K - 1, cwid), lambda i, j: (i, 0, 0))],
        out_shape=[jax.ShapeDtypeStruct((b, s, d), F32),
                   jax.ShapeDtypeStruct((b, s, d), BF16),
                   jax.ShapeDtypeStruct((b, RET_HEADS, RET_DH, RET_DH), F32),
                   jax.ShapeDtypeStruct((b, CONV_K - 1, cwid), F32)],
        scratch_shapes=[pltpu.VMEM((RET_HEADS, RET_DH, RET_DH), F32),
                        pltpu.VMEM((tt + 8, cwid), F32),
                        pltpu.VMEM((tt, rw + cwid), BF16)],
        compiler_params=_cparams(("parallel", "arbitrary")),
        name="mix",
    )(x, mod, cos, sin, dec, cdm, kdm, g128, w_in, w_out,
      g_pre1.reshape(1, d), g_post1.reshape(1, d), g_pre2.reshape(1, d),
      conv_w, conv_b.reshape(1, cwid))


SMP_BLOCK = 8


def _ret_gamma(hh):
    return 1.0 - 2.0 ** (-5.0 - hh)


def _smp_proj_kernel(x_ref, mod_ref, cos_ref, sin_ref, win_ref, gpre1_ref, cw_ref, cb_ref, cache_ref,
                     q_ref, k_ref, v_ref, g_ref, cout_ref, cache_out_ref):
    d = x_ref.shape[1]
    rw = RET_HEADS * RET_DH
    cwid = cw_ref.shape[1]
    x = x_ref[...]
    sh1 = mod_ref[:, 0:d]
    sc1 = mod_ref[:, d:2 * d]
    h = (_rms(x, gpre1_ref[...]) * (1.0 + sc1) + sh1).astype(BF16)
    p = jnp.dot(h, win_ref[...], preferred_element_type=F32)
    cosb, sinb = cos_ref[...], sin_ref[...]
    for hh in range(RET_HEADS):
        l0 = hh * RET_DH
        q_ref[:, l0:l0 + RET_DH] = _rotary(p[:, l0:l0 + RET_DH], cosb, sinb)
        k_ref[:, l0:l0 + RET_DH] = _rotary(p[:, rw + l0:rw + l0 + RET_DH], cosb, sinb) * (RET_DH ** -0.5)
    v_ref[...] = p[:, 2 * rw:3 * rw]
    g_ref[...] = p[:, 3 * rw:4 * rw]
    pb = p[:, 4 * rw:4 * rw + cwid]
    u = p[:, 4 * rw + cwid:4 * rw + 2 * cwid] * p[:, 4 * rw + 2 * cwid:4 * rw + 3 * cwid]
    buf0 = cache_ref[:, 0:cwid]
    buf1 = cache_ref[:, cwid:2 * cwid]
    y = cb_ref[...] + cw_ref[0:1, :] * buf0 + cw_ref[1:2, :] * buf1 + cw_ref[2:3, :] * u
    cout_ref[...] = pb * y
    cache_out_ref[:, 0:cwid] = buf1
    cache_out_ref[:, cwid:2 * cwid] = u


def _smp_state_kernel(q_ref, k_ref, v_ref, s0_ref, o_ref, snew_ref):
    nb = SMP_BLOCK
    row = lax.broadcasted_iota(jnp.int32, (nb, RET_DH), 0)
    zpad = jnp.zeros((RET_CHUNK - nb, RET_DH), F32)
    for hh in range(RET_HEADS):
        l0 = hh * RET_DH
        gam = _ret_gamma(hh)
        qb = q_ref[:, l0:l0 + RET_DH].astype(BF16)
        kb = k_ref[:, l0:l0 + RET_DH].astype(BF16)
        vb = v_ref[:, l0:l0 + RET_DH].astype(BF16)
        s_list = [s0_ref[n, hh] for n in range(nb)]
        scat = jnp.concatenate(s_list, axis=1).astype(BF16)
        res = jnp.dot(qb, scat, preferred_element_type=F32)
        cross = jnp.zeros((nb, RET_DH), F32)
        for n in range(nb):
            cross = cross + jnp.where(row == n, res[:, n * RET_DH:(n + 1) * RET_DH], 0.0)
        sc = jnp.sum(qb.astype(F32) * kb.astype(F32), axis=-1, keepdims=True)
        o_ref[:, l0:l0 + RET_DH] = sc.astype(BF16).astype(F32) * vb.astype(F32) + cross * gam
        kpad = jnp.concatenate([kb.astype(F32), zpad], axis=0).astype(BF16)
        vf = vb.astype(F32)
        w = jnp.concatenate([jnp.where(row == n, vf, 0.0) for n in range(nb)], axis=1)
        wpad = jnp.concatenate([w, jnp.zeros((RET_CHUNK - nb, nb * RET_DH), F32)], axis=0).astype(BF16)
        kv = lax.dot_general(kpad, wpad, TN, preferred_element_type=F32)
        for n in range(nb):
            snew_ref[n, hh] = gam * s_list[n] + kv[:, n * RET_DH:(n + 1) * RET_DH]


def _smp_post_kernel(x_ref, mod_ref, o_ref, g_ref, cout_ref, wout_ref, gpost1_ref, gpre2_ref,
                     x1_ref, h2_ref):
    d = x_ref.shape[1]
    rw = RET_HEADS * RET_DH
    x = x_ref[...]
    gt1 = mod_ref[:, 2 * d:3 * d]
    sh2 = mod_ref[:, 3 * d:4 * d]
    sc2 = mod_ref[:, 4 * d:5 * d]
    m = jnp.dot(cout_ref[...].astype(BF16), wout_ref[rw:, :], preferred_element_type=F32)
    for hh in range(RET_HEADS):
        l0 = hh * RET_DH
        o = o_ref[:, l0:l0 + RET_DH]
        oc = o - jnp.mean(o, axis=-1, keepdims=True)
        on = oc * lax.rsqrt(jnp.mean(oc * oc, axis=-1, keepdims=True) + EPS)
        r = (_silu(g_ref[:, l0:l0 + RET_DH]) * on).astype(BF16)
        m = m + jnp.dot(r, wout_ref[l0:l0 + RET_DH, :], preferred_element_type=F32)
    x1 = x + gt1 * _rms(m, gpost1_ref[...])
    x1_ref[...] = x1
    h2_ref[...] = (_rms(x1, gpre2_ref[...]) * (1.0 + sc2) + sh2).astype(BF16)


def _sample_mixer(x, mod, state, cache, pos, w_in, w_out, g_pre1, g_post1, g_pre2, conv_w, conv_b):
    n, d = x.shape
    rw = RET_HEADS * RET_DH
    cwid = conv_w.shape[1]
    cos, sin = _rope_tables(pos)
    f = jax.ShapeDtypeStruct
    q, k, v, g, cout, cache_out = pl.pallas_call(
        _smp_proj_kernel,
        out_shape=[f((n, rw), F32)] * 4 + [f((n, cwid), F32), f((n, (CONV_K - 1) * cwid), F32)],
        compiler_params=pltpu.CompilerParams(vmem_limit_bytes=VMEM_LIMIT),
        name="smp_proj",
    )(x, mod, cos, sin, w_in, g_pre1.reshape(1, d), conv_w, conv_b.reshape(1, cwid),
      cache.reshape(n, (CONV_K - 1) * cwid))
    nb = SMP_BLOCK
    vec = pl.BlockSpec((nb, rw), lambda i: (i, 0))
    st = pl.BlockSpec((nb, RET_HEADS, RET_DH, RET_DH), lambda i: (i, 0, 0, 0))
    o, snew = pl.pallas_call(
        _smp_state_kernel,
        grid=(n // nb,),
        in_specs=[vec, vec, vec, st],
        out_specs=[vec, st],
        out_shape=[f((n, rw), F32), f(state.shape, F32)],
        compiler_params=_cparams(("parallel",)),
        name="smp_state",
    )(q, k, v, state)
    x1, h2 = pl.pallas_call(
        _smp_post_kernel,
        out_shape=[f((n, d), F32), f((n, d), BF16)],
        compiler_params=pltpu.CompilerParams(vmem_limit_bytes=VMEM_LIMIT),
        name="smp_post",
    )(x, mod, o, g, cout, w_out, g_post1.reshape(1, d), g_pre2.reshape(1, d))
    return x1, h2, snew, cache_out.reshape(n, CONV_K - 1, cwid)


_CAND = [(a, b) for a in range(PEER_TOPK) for b in range(PEER_TOPK) if (a + 1) * (b + 1) <= PEER_TOPK]


def _sel_kernel(h2_ref, wq_ref, keys_ref, e1_ref, cnt_ref, e2_ref, r2_ref,
                s1_scr, s2_scr, a_scr, b_scr, *, tt):
    ntc = tt // LANES
    neg = -jnp.inf
    q = jnp.dot(h2_ref[...], wq_ref[...], preferred_element_type=F32).astype(BF16)
    for h in range(PEER_HEADS):
        for c, scr in ((0, s1_scr), (1, s2_scr)):
            col = (h * 2 + c) * PEER_HALF
            scr[h] = lax.dot_general(keys_ref[h, c], q[:, col:col + PEER_HALF], NT,
                                     preferred_element_type=F32)

    def token_chunk(tc, carry):
        lanes = pl.ds(pl.multiple_of(tc * LANES, LANES), LANES)
        for h in range(PEER_HEADS):
            for side, (s_scr, v_scr) in enumerate(((s1_scr, a_scr), (s2_scr, b_scr))):
                s = s_scr[h, :, lanes]
                rank = jnp.full(s.shape, float(PEER_TOPK), F32)
                for k in range(PEER_TOPK):
                    m = jnp.max(s, axis=0, keepdims=True)
                    hit = s == m
                    if side == 1:
                        rank = jnp.where(hit, float(k), rank)
                    s = jnp.where(hit, neg, s)
                    v_scr[k, h:h + 1, lanes] = m
                if side == 1:
                    r2_ref[h, :, lanes] = rank.astype(BF16)
        av = [a_scr[k, :, lanes] for k in range(PEER_TOPK)]
        bv = [b_scr[k, :, lanes] for k in range(PEER_TOPK)]
        cands = [av[a] + bv[b] for a, b in _CAND]
        cur = list(cands)
        t = None
        for r in range(PEER_TOPK):
            t = functools.reduce(jnp.maximum, cur)
            if r + 1 < PEER_TOPK:
                cur = [jnp.where(c == t, neg, c) for c in cur]
        z = None
        for c in cands:
            term = jnp.where(c >= t, jnp.exp(c - cands[0]), 0.0)
            z = term if z is None else z + term
        zi = 1.0 / z
        for h in range(PEER_HEADS):
            s1 = s1_scr[h, :, lanes]
            s2 = s2_scr[h, :, lanes]
            th = t[h:h + 1]
            cnt = jnp.zeros(s1.shape, F32)
            for k in range(PEER_TOPK):
                cnt = cnt + jnp.where(s1 + bv[k][h:h + 1] >= th, 1.0, 0.0)
            cnt_ref[h, :, lanes] = cnt
            e1_ref[h, :, lanes] = jnp.exp(s1 - av[0][h:h + 1]) * zi[h:h + 1]
            e2_ref[h, :, lanes] = jnp.exp(s2 - bv[0][h:h + 1]).astype(BF16)
        return carry

    lax.fori_loop(0, ntc, token_chunk, 0)


def _peer_select(h2, w_q, keys, tt):
    t, d = h2.shape
    shape = (PEER_HEADS, PEER_NKEYS, t)
    out = [jax.ShapeDtypeStruct(shape, F32)] * 2 + [jax.ShapeDtypeStruct(shape, BF16)] * 2
    ospec = pl.BlockSpec((PEER_HEADS, PEER_NKEYS, tt), lambda i: (0, 0, i))
    return pl.pallas_call(
        functools.partial(_sel_kernel, tt=tt),
        grid=(t // tt,),
        in_specs=[pl.BlockSpec((tt, d), lambda i: (i, 0)),
                  pl.BlockSpec(w_q.shape, lambda i: (0, 0)),
                  pl.BlockSpec(keys.shape, lambda i: (0, 0, 0, 0))],
        out_specs=[ospec] * 4,
        out_shape=out,
        scratch_shapes=[pltpu.VMEM((PEER_HEADS, PEER_NKEYS, tt), F32),
                        pltpu.VMEM((PEER_HEADS, PEER_NKEYS, tt), F32),
                        pltpu.VMEM((PEER_TOPK, PEER_HEADS, tt), F32),
                        pltpu.VMEM((PEER_TOPK, PEER_HEADS, tt), F32)],
        compiler_params=_cparams(("parallel",)),
        name="sel",
    )(h2, w_q, keys)


PEER_ECHUNK = 512
PEER_JSUB = 16


def _gelu_tanh_bf16(x):
    c = float(np.sqrt(2.0 / np.pi))
    t = jnp.tanh(x * (c + (0.044715 * c) * (x * x)))
    return x * (0.5 + 0.5 * t)


def _peer_kernel(h2_ref, u_hbm, vt_hbm, e1_ref, cnt_ref, e2_ref, r2_ref, x1_ref, gt2_ref, gpost2_ref,
                 y_ref, ubuf, vbuf, usem, vsem, a_scr, c_scr, acc_ref, er_scr, xt_scr, *, nc):
    tt = h2_ref.shape[0]
    xt_scr[...] = h2_ref[...].T
    acc_ref[...] = jnp.zeros_like(acc_ref)
    r2_off = PEER_HEADS * PEER_NKEYS + PEER_JSUB
    for h in range(PEER_HEADS):
        er_scr[h * PEER_NKEYS:(h + 1) * PEER_NKEYS, 0:tt] = e2_ref[h]
        er_scr[r2_off + h * PEER_NKEYS:r2_off + (h + 1) * PEER_NKEYS, 0:tt] = r2_ref[h]

    def u_copy(c, slot):
        return pltpu.make_async_copy(u_hbm.at[c], ubuf.at[slot], usem.at[slot])

    def v_copy(c, slot):
        return pltpu.make_async_copy(vt_hbm.at[c], vbuf.at[slot], vsem.at[slot])

    def dot1(slot):
        a_scr[slot, :, 0:tt] = jnp.dot(ubuf[slot], xt_scr[...], preferred_element_type=F32)

    def dot2(slot):
        acc_ref[...] += jnp.dot(vbuf[slot], c_scr[slot], preferred_element_type=F32)

    def ew(c, slot):
        ipc = PEER_ECHUNK // PEER_NKEYS
        base = (c // 2) * (2 * ipc)
        if not isinstance(base, int):
            base = pl.multiple_of(base, 2 * ipc)
        for il in range(ipc):
            off = slot * ipc + il
            for lc in range(tt // LANES):
                lanes = slice(lc * LANES, (lc + 1) * LANES)
                bshape = (PEER_JSUB, LANES)
                njc = PEER_NKEYS // PEER_JSUB
                g = [None] * njc
                for h in range(PEER_HEADS):
                    e1b = jnp.broadcast_to(e1_ref[h, pl.ds(base, 2 * ipc), lanes][off:off + 1], bshape).astype(BF16)
                    cntb = jnp.broadcast_to(cnt_ref[h, pl.ds(base, 2 * ipc), lanes][off:off + 1], bshape).astype(BF16)
                    for jc in range(njc):
                        j0 = h * PEER_NKEYS + jc * PEER_JSUB
                        term = jnp.where(er_scr[r2_off + j0:r2_off + j0 + PEER_JSUB, lanes] < cntb,
                                         er_scr[j0:j0 + PEER_JSUB, lanes], 0.0) * e1b
                        g[jc] = term if g[jc] is None else g[jc] + term
                for jc in range(njc):
                    r0 = il * PEER_NKEYS + jc * PEER_JSUB
                    act = _gelu_tanh_bf16(a_scr[slot, r0:r0 + PEER_JSUB, lanes].astype(BF16))
                    c_scr[slot, r0:r0 + PEER_JSUB, lanes] = act * g[jc]

    def stage(s, par):
        static = isinstance(s, int)
        do_dot1 = (not static) or s < nc
        do_ew = (not static) or 1 <= s <= nc
        do_dot2 = (not static) or s >= 2
        if do_dot1:
            u_copy(s, par).wait()
        if do_dot2:
            v_copy(s - 2, par).wait()
        if static:
            if s + 1 < nc:
                u_copy(s + 1, 1 - par).start()
        else:
            @pl.when(s + 1 < nc)
            def _():
                u_copy(s + 1, 1 - par).start()
        if do_ew:
            v_copy(s - 1, 1 - par).start()
        if do_dot1:
            dot1(par)
        if do_dot2:
            dot2(par)
        if do_ew:
            ew(s - 1, 1 - par)

    u_copy(0, 0).start()
    stage(0, 0)
    stage(1, 1)

    def pair(k, carry):
        stage(2 * k, 0)
        stage(2 * k + 1, 1)
        return carry

    lax.fori_loop(1, nc // 2, pair, 0)
    stage(nc, 0)
    stage(nc + 1, 1)

    f = acc_ref[...].T
    gt2 = gt2_ref[0] if len(gt2_ref.shape) == 3 else gt2_ref[...]
    y_ref[...] = x1_ref[...] + gt2 * _rms(f, gpost2_ref[...])


def _peer_dense(h2, x1, gt2, ub, vt, sel, g_post2, tt, seq_len):
    t, d = h2.shape
    nc = ub.shape[0]
    assert nc % 2 == 0 and nc >= 4
    e1, cnt, e2, r2 = sel
    per_t = pl.BlockSpec((PEER_HEADS, PEER_NKEYS, tt), lambda i: (0, 0, i))
    if seq_len == 1:
        gspec = pl.BlockSpec((tt, d), lambda i: (i, 0))
    else:
        assert seq_len % tt == 0
        tiles_per_seq = seq_len // tt
        gspec = pl.BlockSpec((1, 1, d), lambda i: (i // tiles_per_seq, 0, 0))
        gt2 = gt2.reshape(gt2.shape[0], 1, d)
    hbm = pl.BlockSpec(memory_space=pl.ANY)
    return pl.pallas_call(
        functools.partial(_peer_kernel, nc=nc),
        grid=(t // tt,),
        in_specs=[pl.BlockSpec((tt, d), lambda i: (i, 0)),
                  hbm, hbm,
                  per_t, per_t, per_t, per_t,
                  pl.BlockSpec((tt, d), lambda i: (i, 0)),
                  gspec,
                  pl.BlockSpec((1, d), lambda i: (0, 0))],
        out_specs=pl.BlockSpec((tt, d), lambda i: (i, 0)),
        out_shape=jax.ShapeDtypeStruct((t, d), F32),
        scratch_shapes=[pltpu.VMEM((2, PEER_ECHUNK, d), BF16),
                        pltpu.VMEM((2, d, PEER_ECHUNK), BF16),
                        pltpu.SemaphoreType.DMA((2,)),
                        pltpu.SemaphoreType.DMA((2,)),
                        pltpu.VMEM((2, PEER_ECHUNK, tt + LANES), F32),
                        pltpu.VMEM((2, PEER_ECHUNK, tt), BF16),
                        pltpu.VMEM((d, tt), F32),
                        pltpu.VMEM((2 * PEER_HEADS * PEER_NKEYS + PEER_JSUB, tt + LANES), BF16),
                        pltpu.VMEM((d, tt), BF16)],
        compiler_params=_cparams(("arbitrary",)),
        name="peer",
    )(h2, ub, vt, e1, cnt, e2, r2, x1, gt2, g_post2.reshape(1, d))


def _prep_kernel(u_ref, v_ref, ub_ref, vt_ref):
    ub_ref[0] = u_ref[...].astype(BF16)
    vt_ref[0] = v_ref[...].T.astype(BF16)


def _prep_tables(u_tab, v_tab):
    ne, d = u_tab.shape
    blk = PEER_ECHUNK
    return pl.pallas_call(
        _prep_kernel,
        grid=(ne // blk,),
        in_specs=[pl.BlockSpec((blk, d), lambda i: (i, 0)), pl.BlockSpec((blk, d), lambda i: (i, 0))],
        out_specs=[pl.BlockSpec((1, blk, d), lambda i: (i, 0, 0)), pl.BlockSpec((1, d, blk), lambda i: (i, 0, 0))],
        out_shape=[jax.ShapeDtypeStruct((ne // blk, blk, d), BF16),
                   jax.ShapeDtypeStruct((ne // blk, d, blk), BF16)],
        compiler_params=_cparams(("parallel",)),
        name="prep",
    )(u_tab, v_tab)


PAST_LEN = 16384
MIX_TILE = 256
SEL_TILE = 512
PEER_TILE = 512


def kernel(x_prompt, x_sample, c_prompt, c_sample, state_ret, cache_conv, w_ada, b_ada, g_pre1, g_post1, g_pre2, g_post2, w_in, w_out, conv_w, conv_b, w_q, sub_keys, u_tab, v_tab):
    depth = w_in.shape[0]
    assert depth == 1, "single-layer stack"
    l = 0
    b, s, d = x_prompt.shape
    n = x_sample.shape[0]
    assert x_sample.shape[1] == 1
    mod =_modulation(jnp.concatenate([c_prompt, c_sample], axis=0), w_ada[l], b_ada[l])
    mod_p, mod_s = mod[:b], mod[b:]
    w_in_b, w_out_b = w_in[l].astype(BF16), w_out[l].astype(BF16)
    w_q_b, keys_b = w_q[l].astype(BF16), sub_keys[l].astype(BF16)
    ub, vt = _prep_tables(u_tab[l], v_tab[l])

    x1p, h2p, sfin_p, cache_p = _prompt_mixer(
        x_prompt, mod_p.reshape(b, N_MOD, d), w_in_b, w_out_b,
        g_pre1[l], g_post1[l], g_pre2[l], conv_w[l], conv_b[l], MIX_TILE)
    x1s, h2s, sfin_s, cache_s = _sample_mixer(
        x_sample.reshape(n, d), mod_s, state_ret[l], cache_conv[l],
        PAST_LEN + jnp.arange(1, dtype=jnp.int32), w_in_b, w_out_b,
        g_pre1[l], g_post1[l], g_pre2[l], conv_w[l], conv_b[l])

    h2p = h2p.reshape(b * s, d)
    sel_p = _peer_select(h2p, w_q_b, keys_b, min(SEL_TILE, s))
    yp = _peer_dense(h2p, x1p.reshape(b * s, d), mod_p[:, 5 * d:6 * d], ub, vt, sel_p, g_post2[l],
                     min(PEER_TILE, s), s)
    sel_s = _peer_select(h2s, w_q_b, keys_b, n)
    ys = _peer_dense(h2s, x1s, mod_s[:, 5 * d:6 * d], ub, vt, sel_s, g_post2[l], n, 1)

    return (yp.reshape(b, s, d), ys.reshape(n, 1, d), sfin_p[None], cache_p[None],
            sfin_s[None], cache_s[None])
```

```python
import functools

import jax
import jax.numpy as jnp
import numpy as np
from jax import lax
from jax.experimental import pallas as pl
from jax.experimental.pallas import tpu as pltpu

F32 = jnp.float32
BF16 = jnp.bfloat16

LANES = 128
EPS = 1e-6
ROPE_BASE = 10000.0
RET_HEADS = 4
RET_DH = 128
RET_CHUNK = 128
CONV_K = 3
PEER_HEADS = 8
PEER_NKEYS = 128
PEER_HALF = 128
PEER_TOPK = 16
N_MOD = 6
VMEM_LIMIT = 56 * 1024 * 1024

NT = (((1,), (1,)), ((), ()))
TN = (((0,), (0,)), ((), ()))


def _cparams(sem):
    return pltpu.CompilerParams(dimension_semantics=sem, vmem_limit_bytes=VMEM_LIMIT)


def _rms(x, g):
    return x * lax.rsqrt(jnp.mean(x * x, axis=-1, keepdims=True) + EPS) * g


def _silu(x):
    return x / (1.0 + jnp.exp(-x))


def _rotary(x, cos, sin_signed):
    return x * cos + pltpu.roll(x, RET_DH // 2, axis=1) * sin_signed


def _mod_kernel(c_ref, w_ref, b_ref, o_ref):
    a = _silu(c_ref[...]).astype(BF16)
    o_ref[...] = jnp.dot(a, w_ref[...].astype(BF16), preferred_element_type=F32) + b_ref[...]


def _modulation(c, w_ada, b_ada):
    n, d = c.shape
    cols = w_ada.shape[1]
    blk = d
    return pl.pallas_call(
        _mod_kernel,
        grid=(cols // blk,),
        in_specs=[pl.BlockSpec((n, d), lambda j: (0, 0)),
                  pl.BlockSpec((d, blk), lambda j: (0, j)),
                  pl.BlockSpec((1, blk), lambda j: (0, j))],
        out_specs=pl.BlockSpec((n, blk), lambda j: (0, j)),
        out_shape=jax.ShapeDtypeStruct((n, cols), F32),
        compiler_params=_cparams(("parallel",)),
        name="mod",
    )(c, w_ada, b_ada.reshape(1, cols))


def _mix_kernel(x_ref, mod_ref, cos_ref, sin_ref, dec_ref, cdm_ref, kdm_ref, g128_ref,
                win_ref, wout_ref, gpre1_ref, gpost1_ref, gpre2_ref, cw_ref, cb_ref,
                x1_ref, h2_ref, sfin_ref, cache_ref,
                s_scr, ext_scr, mix_scr, *, tt):
    j = pl.program_id(1)
    rw = RET_HEADS * RET_DH
    cwid = cw_ref.shape[1]

    @pl.when(j == 0)
    def _():
        s_scr[...] = jnp.zeros_like(s_scr)
        ext_scr[0:8, :] = jnp.zeros((8, cwid), F32)

    x = x_ref[0]
    mod = mod_ref[0]
    sh1, sc1, gt1, sh2, sc2, gt2 = [mod[i:i + 1] for i in range(N_MOD)]
    h = (_rms(x, gpre1_ref[...]) * (1.0 + sc1) + sh1).astype(BF16)

    def proj(lo, width):
        return jnp.dot(h, win_ref[:, lo:lo + width], preferred_element_type=F32)

    pq = proj(0, rw)
    pk = proj(rw, rw)
    pv = proj(2 * rw, rw)
    pg = proj(3 * rw, rw)

    for c in range(tt // RET_CHUNK):
        r0 = c * RET_CHUNK
        cosb = cos_ref[r0:r0 + RET_CHUNK, :]
        sinb = sin_ref[r0:r0 + RET_CHUNK, :]
        for hh in range(RET_HEADS):
            l0 = hh * RET_DH
            q = _rotary(pq[r0:r0 + RET_CHUNK, l0:l0 + RET_DH], cosb, sinb)
            k = _rotary(pk[r0:r0 + RET_CHUNK, l0:l0 + RET_DH], cosb, sinb) * (RET_DH ** -0.5)
            v = pv[r0:r0 + RET_CHUNK, l0:l0 + RET_DH]
            g = pg[r0:r0 + RET_CHUNK, l0:l0 + RET_DH]
            qb, kb, vb = q.astype(BF16), k.astype(BF16), v.astype(BF16)
            s_old = s_scr[hh]
            scores = lax.dot_general(qb, kb, NT, preferred_element_type=F32) * dec_ref[hh]
            inner = jnp.dot(scores.astype(BF16), vb, preferred_element_type=F32)
            cross = jnp.dot(qb, s_old.astype(BF16), preferred_element_type=F32) * cdm_ref[hh]
            o = inner + cross
            kd = (k * kdm_ref[hh]).astype(BF16)
            s_scr[hh] = g128_ref[hh] * s_old + lax.dot_general(kd, vb, TN, preferred_element_type=F32)
            mu = jnp.mean(o, axis=-1, keepdims=True)
            oc = o - mu
            on = oc * lax.rsqrt(jnp.mean(oc * oc, axis=-1, keepdims=True) + EPS)
            mix_scr[r0:r0 + RET_CHUNK, l0:l0 + RET_DH] = (_silu(g) * on).astype(BF16)

    pb = proj(4 * rw, cwid)
    u = proj(4 * rw + cwid, cwid) * proj(4 * rw + 2 * cwid, cwid)
    ext_scr[8:8 + tt, :] = u
    y = (cb_ref[...] + cw_ref[0:1, :] * ext_scr[6:6 + tt, :]
         + cw_ref[1:2, :] * ext_scr[7:7 + tt, :] + cw_ref[2:3, :] * u)
    mix_scr[:, rw:rw + cwid] = (pb * y).astype(BF16)
    cache_ref[0] = ext_scr[tt + 6:tt + 8, :]
    ext_scr[0:8, :] = ext_scr[tt:tt + 8, :]

    m = jnp.dot(mix_scr[...], wout_ref[...], preferred_element_type=F32)
    x1 = x + gt1 * _rms(m, gpost1_ref[...])
    x1_ref[0] = x1
    h2_ref[0] = (_rms(x1, gpre2_ref[...]) * (1.0 + sc2) + sh2).astype(BF16)

    @pl.when(j == pl.num_programs(1) - 1)
    def _():
        sfin_ref[0] = s_scr[...]


def _decay_tables():
    lg = jnp.log(1.0 - 2.0 ** (-5.0 - jnp.arange(RET_HEADS, dtype=F32)))
    L = RET_CHUNK
    idx = jnp.arange(L, dtype=F32)
    rel = idx[:, None] - idx[None, :]
    dec = jnp.where(rel[None] >= 0, jnp.exp(lg[:, None, None] * jnp.maximum(rel, 0.0)[None]), 0.0)
    ones = jnp.ones((1, 1, RET_DH), F32)
    cdm = jnp.exp(lg[:, None] * (idx[None, :] + 1.0))[:, :, None] * ones
    kdm = jnp.exp(lg[:, None] * (L - 1.0 - idx[None, :]))[:, :, None] * ones
    g128 = jnp.exp(lg * L)[:, None, None] * jnp.ones((1, L, RET_DH), F32)
    return dec, cdm, kdm, g128


def _rope_tables(pos):
    half = RET_DH // 2
    freqs = ROPE_BASE ** (-jnp.arange(half, dtype=F32) / half)
    ang = pos.astype(F32)[:, None] * freqs[None, :]
    cos, sin = jnp.cos(ang), jnp.sin(ang)
    return jnp.concatenate([cos, cos], axis=-1), jnp.concatenate([-sin, sin], axis=-1)


def _prompt_mixer(x, mod, w_in, w_out, g_pre1, g_post1, g_pre2, conv_w, conv_b, tt):
    b, s, d = x.shape
    rw = RET_HEADS * RET_DH
    cwid = conv_w.shape[1]
    cos, sin = _rope_tables(jnp.arange(s, dtype=jnp.int32))
    dec, cdm, kdm, g128 = _decay_tables()
    tab = pl.BlockSpec((RET_HEADS, RET_CHUNK, RET_DH), lambda i, j: (0, 0, 0))
    row = pl.BlockSpec((1, d), lambda i, j: (0, 0))
    return pl.pallas_call(
        functools.partial(_mix_kernel, tt=tt),
        grid=(b, s // tt),
        in_specs=[pl.BlockSpec((1, tt, d), lambda i, j: (i, j, 0)),
                  pl.BlockSpec((1, N_MOD, d), lambda i, j: (i, 0, 0)),
                  pl.BlockSpec((tt, RET_DH), lambda i, j: (j, 0)),
                  pl.BlockSpec((tt, RET_DH), lambda i, j: (j, 0)),
                  tab, tab, tab, tab,
                  pl.BlockSpec(w_in.shape, lambda i, j: (0, 0)),
                  pl.BlockSpec(w_out.shape, lambda i, j: (0, 0)),
                  row, row, row,
                  pl.BlockSpec((CONV_K, cwid), lambda i, j: (0, 0)),
                  pl.BlockSpec((1, cwid), lambda i, j: (0, 0))],
        out_specs=[pl.BlockSpec((1, tt, d), lambda i, j: (i, j, 0)),
                   pl.BlockSpec((1, tt, d), lambda i, j: (i, j, 0)),
                   pl.BlockSpec((1, RET_HEADS, RET_DH, RET_DH), lambda i, j: (i, 0, 0, 0)),
                   pl.BlockSpec((1, CONV_K - 1, cwid), lambda i, j: (i, 0, 0))],
        out_shape=[jax.ShapeDtypeStruct((b, s, d), F32),
                   jax.ShapeDtypeStruct((b, s, d), BF16),
                   jax.ShapeDtypeStruct((b, RET_HEADS, RET_DH, RET_DH), F32),
                   jax.ShapeDtypeStruct((b, CONV_K - 1, cwid), F32)],
        scratch_shapes=[pltpu.VMEM((RET_HEADS, RET_DH, RET_DH), F32),
                        pltpu.VMEM((tt + 8, cwid), F32),
                        pltpu.VMEM((tt, rw + cwid), BF16)],
        compiler_params=_cparams(("parallel", "arbitrary")),
        name="mix",
    )(x, mod, cos, sin, dec, cdm, kdm, g128, w_in, w_out,
      g_pre1.reshape(1, d), g_post1.reshape(1, d), g_pre2.reshape(1, d),
      conv_w, conv_b.reshape(1, cwid))


SMP_BLOCK = 8


def _ret_gamma(hh):
    return 1.0 - 2.0 ** (-5.0 - hh)


def _smp_proj_kernel(x_ref, mod_ref, cos_ref, sin_ref, win_ref, gpre1_ref, cw_ref, cb_ref, cache_ref,
                     q_ref, k_ref, v_ref, g_ref, cout_ref, cache_out_ref):
    d = x_ref.shape[1]
    rw = RET_HEADS * RET_DH
    cwid = cw_ref.shape[1]
    x = x_ref[...]
    sh1 = mod_ref[:, 0:d]
    sc1 = mod_ref[:, d:2 * d]
    h = (_rms(x, gpre1_ref[...]) * (1.0 + sc1) + sh1).astype(BF16)
    p = jnp.dot(h, win_ref[...], preferred_element_type=F32)
    cosb, sinb = cos_ref[...], sin_ref[...]
    for hh in range(RET_HEADS):
        l0 = hh * RET_DH
        q_ref[:, l0:l0 + RET_DH] = _rotary(p[:, l0:l0 + RET_DH], cosb, sinb)
        k_ref[:, l0:l0 + RET_DH] = _rotary(p[:, rw + l0:rw + l0 + RET_DH], cosb, sinb) * (RET_DH ** -0.5)
    v_ref[...] = p[:, 2 * rw:3 * rw]
    g_ref[...] = p[:, 3 * rw:4 * rw]
    pb = p[:, 4 * rw:4 * rw + cwid]
    u = p[:, 4 * rw + cwid:4 * rw + 2 * cwid] * p[:, 4 * rw + 2 * cwid:4 * rw + 3 * cwid]
    buf0 = cache_ref[:, 0:cwid]
    buf1 = cache_ref[:, cwid:2 * cwid]
    y = cb_ref[...] + cw_ref[0:1, :] * buf0 + cw_ref[1:2, :] * buf1 + cw_ref[2:3, :] * u
    cout_ref[...] = pb * y
    cache_out_ref[:, 0:cwid] = buf1
    cache_out_ref[:, cwid:2 * cwid] = u


def _smp_state_kernel(q_ref, k_ref, v_ref, s0_ref, o_ref, snew_ref):
    nb = SMP_BLOCK
    row = lax.broadcasted_iota(jnp.int32, (nb, RET_DH), 0)
    zpad = jnp.zeros((RET_CHUNK - nb, RET_DH), F32)
    for hh in range(RET_HEADS):
        l0 = hh * RET_DH
        gam = _ret_gamma(hh)
        qb = q_ref[:, l0:l0 + RET_DH].astype(BF16)
        kb = k_ref[:, l0:l0 + RET_DH].astype(BF16)
        vb = v_ref[:, l0:l0 + RET_DH].astype(BF16)
        s_list = [s0_ref[n, hh] for n in range(nb)]
        scat = jnp.concatenate(s_list, axis=1).astype(BF16)
        res = jnp.dot(qb, scat, preferred_element_type=F32)
        cross = jnp.zeros((nb, RET_DH), F32)
        for n in range(nb):
            cross = cross + jnp.where(row == n, res[:, n * RET_DH:(n + 1) * RET_DH], 0.0)
        sc = jnp.sum(qb.astype(F32) * kb.astype(F32), axis=-1, keepdims=True)
        o_ref[:, l0:l0 + RET_DH] = sc.astype(BF16).astype(F32) * vb.astype(F32) + cross * gam
        kpad = jnp.concatenate([kb.astype(F32), zpad], axis=0).astype(BF16)
        vf = vb.astype(F32)
        w = jnp.concatenate([jnp.where(row == n, vf, 0.0) for n in range(nb)], axis=1)
        wpad = jnp.concatenate([w, jnp.zeros((RET_CHUNK - nb, nb * RET_DH), F32)], axis=0).astype(BF16)
        kv = lax.dot_general(kpad, wpad, TN, preferred_element_type=F32)
        for n in range(nb):
            snew_ref[n, hh] = gam * s_list[n] + kv[:, n * RET_DH:(n + 1) * RET_DH]


def _smp_post_kernel(x_ref, mod_ref, o_ref, g_ref, cout_ref, wout_ref, gpost1_ref, gpre2_ref,
                     x1_ref, h2_ref):
    d = x_ref.shape[1]
    rw = RET_HEADS * RET_DH
    x = x_ref[...]
    gt1 = mod_ref[:, 2 * d:3 * d]
    sh2 = mod_ref[:, 3 * d:4 * d]
    sc2 = mod_ref[:, 4 * d:5 * d]
    m = jnp.dot(cout_ref[...].astype(BF16), wout_ref[rw:, :], preferred_element_type=F32)
    for hh in range(RET_HEADS):
        l0 = hh * RET_DH
        o = o_ref[:, l0:l0 + RET_DH]
        oc = o - jnp.mean(o, axis=-1, keepdims=True)
        on = oc * lax.rsqrt(jnp.mean(oc * oc, axis=-1, keepdims=True) + EPS)
        r = (_silu(g_ref[:, l0:l0 + RET_DH]) * on).astype(BF16)
        m = m + jnp.dot(r, wout_ref[l0:l0 + RET_DH, :], preferred_element_type=F32)
    x1 = x + gt1 * _rms(m, gpost1_ref[...])
    x1_ref[...] = x1
    h2_ref[...] = (_rms(x1, gpre2_ref[...]) * (1.0 + sc2) + sh2).astype(BF16)


def _sample_mixer(x, mod, state, cache, pos, w_in, w_out, g_pre1, g_post1, g_pre2, conv_w, conv_b):
    n, d = x.shape
    rw = RET_HEADS * RET_DH
    cwid = conv_w.shape[1]
    cos, sin = _rope_tables(pos)
    f = jax.ShapeDtypeStruct
    q, k, v, g, cout, cache_out = pl.pallas_call(
        _smp_proj_kernel,
        out_shape=[f((n, rw), F32)] * 4 + [f((n, cwid), F32), f((n, (CONV_K - 1) * cwid), F32)],
        compiler_params=pltpu.CompilerParams(vmem_limit_bytes=VMEM_LIMIT),
        name="smp_proj",
    )(x, mod, cos, sin, w_in, g_pre1.reshape(1, d), conv_w, conv_b.reshape(1, cwid),
      cache.reshape(n, (CONV_K - 1) * cwid))
    nb = SMP_BLOCK
    vec = pl.BlockSpec((nb, rw), lambda i: (i, 0))
    st = pl.BlockSpec((nb, RET_HEADS, RET_DH, RET_DH), lambda i: (i, 0, 0, 0))
    o, snew = pl.pallas_call(
        _smp_state_kernel,
        grid=(n // nb,),
        in_specs=[vec, vec, vec, st],
        out_specs=[vec, st],
        out_shape=[f((n, rw), F32), f(state.shape, F32)],
        compiler_params=_cparams(("parallel",)),
        name="smp_state",
    )(q, k, v, state)
    x1, h2 = pl.pallas_call(
        _smp_post_kernel,
        out_shape=[f((n, d), F32), f((n, d), BF16)],
        compiler_params=pltpu.CompilerParams(vmem_limit_bytes=VMEM_LIMIT),
        name="smp_post",
    )(x, mod, o, g, cout, w_out, g_post1.reshape(1, d), g_pre2.reshape(1, d))
    return x1, h2, snew, cache_out.reshape(n, CONV_K - 1, cwid)


_CAND = [(a, b) for a in range(PEER_TOPK) for b in range(PEER_TOPK) if (a + 1) * (b + 1) <= PEER_TOPK]


def _sel_kernel(h2_ref, wq_ref, keys_ref, e1_ref, cnt_ref, e2_ref, r2_ref,
                s1_scr, s2_scr, a_scr, b_scr, *, tt):
    ntc = tt // LANES
    neg = -jnp.inf
    q = jnp.dot(h2_ref[...], wq_ref[...], preferred_element_type=F32).astype(BF16)
    for h in range(PEER_HEADS):
        for c, scr in ((0, s1_scr), (1, s2_scr)):
            col = (h * 2 + c) * PEER_HALF
            scr[h] = lax.dot_general(keys_ref[h, c], q[:, col:col + PEER_HALF], NT,
                                     preferred_element_type=F32)

    def token_chunk(tc, carry):
        lanes = pl.ds(pl.multiple_of(tc * LANES, LANES), LANES)
        for h in range(PEER_HEADS):
            for side, (s_scr, v_scr) in enumerate(((s1_scr, a_scr), (s2_scr, b_scr))):
                s = s_scr[h, :, lanes]
                rank = jnp.full(s.shape, float(PEER_TOPK), F32)
                for k in range(PEER_TOPK):
                    m = jnp.max(s, axis=0, keepdims=True)
                    hit = s == m
                    if side == 1:
                        rank = jnp.where(hit, float(k), rank)
                    s = jnp.where(hit, neg, s)
                    v_scr[k, h:h + 1, lanes] = m
                if side == 1:
                    r2_ref[h, :, lanes] = rank.astype(BF16)
        av = [a_scr[k, :, lanes] for k in range(PEER_TOPK)]
        bv = [b_scr[k, :, lanes] for k in range(PEER_TOPK)]
        cands = [av[a] + bv[b] for a, b in _CAND]
        cur = list(cands)
        t = None
        for r in range(PEER_TOPK):
            t = functools.reduce(jnp.maximum, cur)
            if r + 1 < PEER_TOPK:
                cur = [jnp.where(c == t, neg, c) for c in cur]
        z = None
        for c in cands:
            term = jnp.where(c >= t, jnp.exp(c - cands[0]), 0.0)
            z = term if z is None else z + term
        zi = 1.0 / z
        for h in range(PEER_HEADS):
            s1 = s1_scr[h, :, lanes]
            s2 = s2_scr[h, :, lanes]
            th = t[h:h + 1]
            cnt = jnp.zeros(s1.shape, F32)
            for k in range(PEER_TOPK):
                cnt = jnp.where(s1 + bv[k][h:h + 1] >= th, float(k + 1), cnt)
            cnt_ref[h, :, lanes] = cnt
            e1_ref[h, :, lanes] = jnp.exp(s1 - av[0][h:h + 1]) * zi[h:h + 1]
            e2_ref[h, :, lanes] = jnp.exp(s2 - bv[0][h:h + 1]).astype(BF16)
        return carry

    lax.fori_loop(0, ntc, token_chunk, 0)


def _peer_select(h2, w_q, keys, tt):
    t, d = h2.shape
    shape = (PEER_HEADS, PEER_NKEYS, t)
    out = [jax.ShapeDtypeStruct(shape, F32)] * 2 + [jax.ShapeDtypeStruct(shape, BF16)] * 2
    ospec = pl.BlockSpec((PEER_HEADS, PEER_NKEYS, tt), lambda i: (0, 0, i))
    return pl.pallas_call(
        functools.partial(_sel_kernel, tt=tt),
        grid=(t // tt,),
        in_specs=[pl.BlockSpec((tt, d), lambda i: (i, 0)),
                  pl.BlockSpec(w_q.shape, lambda i: (0, 0)),
                  pl.BlockSpec(keys.shape, lambda i: (0, 0, 0, 0))],
        out_specs=[ospec] * 4,
        out_shape=out,
        scratch_shapes=[pltpu.VMEM((PEER_HEADS, PEER_NKEYS, tt), F32),
                        pltpu.VMEM((PEER_HEADS, PEER_NKEYS, tt), F32),
                        pltpu.VMEM((PEER_TOPK, PEER_HEADS, tt), F32),
                        pltpu.VMEM((PEER_TOPK, PEER_HEADS, tt), F32)],
        compiler_params=_cparams(("parallel",)),
        name="sel",
    )(h2, w_q, keys)


PEER_ECHUNK = 512
PEER_JSUB = 16
PEER_IGROUP = 2


def _gelu_tanh_bf16(x):
    c = np.sqrt(2.0 / np.pi)
    c0 = jnp.full(x.shape, c, F32).astype(x.dtype)
    c1 = jnp.full(x.shape, 0.044715 * c, F32).astype(x.dtype)
    t = jnp.tanh(x * (c0 + c1 * (x * x)))
    return x * (0.5 + 0.5 * t)


def _peer_kernel(h2_ref, u_ref, vt_ref, e1_ref, cnt_ref, e2_ref, r2_ref, x1_ref, gt2_ref, gpost2_ref,
                 y_ref, acc_ref, c_scr, er_scr, xt_scr, *, eb):
    e = pl.program_id(1)
    tt = h2_ref.shape[0]
    ipc = PEER_ECHUNK // PEER_NKEYS
    r2_off = PEER_HEADS * PEER_NKEYS + PEER_JSUB

    @pl.when(e == 0)
    def _():
        acc_ref[...] = jnp.zeros_like(acc_ref)
        xt_scr[...] = h2_ref[...].T
        for h in range(PEER_HEADS):
            er_scr[h * PEER_NKEYS:(h + 1) * PEER_NKEYS, 0:tt] = e2_ref[h].astype(F32)
            er_scr[r2_off + h * PEER_NKEYS:r2_off + (h + 1) * PEER_NKEYS, 0:tt] = r2_ref[h].astype(F32)

    bshape = (PEER_JSUB, LANES)
    part = None
    for c in range(eb // PEER_ECHUNK):
        c0 = c * PEER_ECHUNK
        a = jnp.dot(u_ref[c0:c0 + PEER_ECHUNK, :], xt_scr[...], preferred_element_type=F32)
        for lc in range(tt // LANES):
            lanes = slice(lc * LANES, (lc + 1) * LANES)
            for ig in range(ipc // PEER_IGROUP):
                ils = [ig * PEER_IGROUP + t for t in range(PEER_IGROUP)]
                e1b, cntb = [], []
                for h in range(PEER_HEADS):
                    rows = [c * ipc + il for il in ils]
                    e1b.append([jnp.broadcast_to(e1_ref[h, i:i + 1, lanes], bshape).astype(BF16) for i in rows])
                    cntb.append([jnp.broadcast_to(cnt_ref[h, i:i + 1, lanes], bshape).astype(BF16) for i in rows])
                for jc in range(PEER_NKEYS // PEER_JSUB):
                    g = [None] * PEER_IGROUP
                    for h in range(PEER_HEADS):
                        j0 = h * PEER_NKEYS + jc * PEER_JSUB
                        r2v = er_scr[r2_off + j0:r2_off + j0 + PEER_JSUB, lanes].astype(BF16)
                        e2v = er_scr[j0:j0 + PEER_JSUB, lanes].astype(BF16)
                        for t in range(PEER_IGROUP):
                            term = jnp.where(r2v < cntb[h][t], e2v, 0.0) * e1b[h][t]
                            g[t] = term if g[t] is None else g[t] + term
                    for t, il in enumerate(ils):
                        r0 = il * PEER_NKEYS + jc * PEER_JSUB
                        act = _gelu_tanh_bf16(a[r0:r0 + PEER_JSUB, lanes].astype(BF16))
                        c_scr[c0 + r0:c0 + r0 + PEER_JSUB, lanes] = act * g[t]
        d2 = jnp.dot(vt_ref[:, c0:c0 + PEER_ECHUNK], c_scr[c0:c0 + PEER_ECHUNK, :],
                     preferred_element_type=F32)
        part = d2 if part is None else part + d2
    acc_ref[...] += part

    @pl.when(e == pl.num_programs(1) - 1)
    def _():
        f = acc_ref[...].T
        gt2 = gt2_ref[0] if len(gt2_ref.shape) == 3 else gt2_ref[...]
        y_ref[...] = x1_ref[...] + gt2 * _rms(f, gpost2_ref[...])


def _peer_dense(h2, x1, gt2, ub, vt, sel, g_post2, tt, eb, seq_len):
    t, d = h2.shape
    ne = ub.shape[0]
    e1, cnt, e2, r2 = sel
    ib = eb // PEER_NKEYS
    per_e = pl.BlockSpec((PEER_HEADS, ib, tt), lambda i, e: (0, e, i))
    per_t = pl.BlockSpec((PEER_HEADS, PEER_NKEYS, tt), lambda i, e: (0, 0, i))
    if seq_len == 1:
        gspec = pl.BlockSpec((tt, d), lambda i, e: (i, 0))
    else:
        assert seq_len % tt == 0
        tiles_per_seq = seq_len // tt
        gspec = pl.BlockSpec((1, 1, d), lambda i, e: (i // tiles_per_seq, 0, 0))
        gt2 = gt2.reshape(gt2.shape[0], 1, d)
    return pl.pallas_call(
        functools.partial(_peer_kernel, eb=eb),
        grid=(t // tt, ne // eb),
        in_specs=[pl.BlockSpec((tt, d), lambda i, e: (i, 0)),
                  pl.BlockSpec((eb, d), lambda i, e: (e, 0)),
                  pl.BlockSpec((d, eb), lambda i, e: (0, e)),
                  per_e, per_e, per_t, per_t,
                  pl.BlockSpec((tt, d), lambda i, e: (i, 0)),
                  gspec,
                  pl.BlockSpec((1, d), lambda i, e: (0, 0))],
        out_specs=pl.BlockSpec((tt, d), lambda i, e: (i, 0)),
        out_shape=jax.ShapeDtypeStruct((t, d), F32),
        scratch_shapes=[pltpu.VMEM((d, tt), F32), pltpu.VMEM((eb, tt), BF16),
                        pltpu.VMEM((2 * PEER_HEADS * PEER_NKEYS + PEER_JSUB, tt + LANES), F32),
                        pltpu.VMEM((d, tt), BF16)],
        compiler_params=_cparams(("parallel", "arbitrary")),
        name="peer",
    )(h2, ub, vt, e1, cnt, e2, r2, x1, gt2, g_post2.reshape(1, d))


def _prep_kernel(u_ref, v_ref, ub_ref, vt_ref):
    ub_ref[...] = u_ref[...].astype(BF16)
    vt_ref[...] = v_ref[...].T.astype(BF16)


def _prep_tables(u_tab, v_tab, blk=512):
    ne, d = u_tab.shape
    return pl.pallas_call(
        _prep_kernel,
        grid=(ne // blk,),
        in_specs=[pl.BlockSpec((blk, d), lambda i: (i, 0)), pl.BlockSpec((blk, d), lambda i: (i, 0))],
        out_specs=[pl.BlockSpec((blk, d), lambda i: (i, 0)), pl.BlockSpec((d, blk), lambda i: (0, i))],
        out_shape=[jax.ShapeDtypeStruct((ne, d), BF16), jax.ShapeDtypeStruct((d, ne), BF16)],
        compiler_params=_cparams(("parallel",)),
        name="prep",
    )(u_tab, v_tab)


PAST_LEN = 16384
MIX_TILE = 512
SEL_TILE = 512
PEER_TILE = 512
PEER_EBLOCK = 2048


def kernel(x_prompt, x_sample, c_prompt, c_sample, state_ret, cache_conv, w_ada, b_ada, g_pre1, g_post1, g_pre2, g_post2, w_in, w_out, conv_w, conv_b, w_q, sub_keys, u_tab, v_tab):
    depth = w_in.shape[0]
    assert depth == 1, "single-layer stack"
    l = 0
    b, s, d = x_prompt.shape
    n = x_sample.shape[0]
    assert x_sample.shape[1] == 1
    mod = _modulation(jnp.concatenate([c_prompt, c_sample], axis=0), w_ada[l], b_ada[l])
    mod_p, mod_s = mod[:b], mod[b:]
    w_in_b, w_out_b = w_in[l].astype(BF16), w_out[l].astype(BF16)
    w_q_b, keys_b = w_q[l].astype(BF16), sub_keys[l].astype(BF16)
    ub, vt = _prep_tables(u_tab[l], v_tab[l])

    x1p, h2p, sfin_p, cache_p = _prompt_mixer(
        x_prompt, mod_p.reshape(b, N_MOD, d), w_in_b, w_out_b,
        g_pre1[l], g_post1[l], g_pre2[l], conv_w[l], conv_b[l], min(MIX_TILE, s))
    x1s, h2s, sfin_s, cache_s = _sample_mixer(
        x_sample.reshape(n, d), mod_s, state_ret[l], cache_conv[l],
        PAST_LEN + jnp.arange(1, dtype=jnp.int32), w_in_b, w_out_b,
        g_pre1[l], g_post1[l], g_pre2[l], conv_w[l], conv_b[l])

    h2p = h2p.reshape(b * s, d)
    sel_p = _peer_select(h2p, w_q_b, keys_b, min(SEL_TILE, s))
    yp = _peer_dense(h2p, x1p.reshape(b * s, d), mod_p[:, 5 * d:6 * d], ub, vt, sel_p, g_post2[l],
                     min(PEER_TILE, s), PEER_EBLOCK, s)
    sel_s = _peer_select(h2s, w_q_b, keys_b, n)
    ys = _peer_dense(h2s, x1s, mod_s[:, 5 * d:6 * d], ub, vt, sel_s, g_post2[l], n, PEER_EBLOCK, 1)

    return (yp.reshape(b, s, d), ys.reshape(n, 1, d), sfin_p[None], cache_p[None],
            sfin_s[None], cache_s[None])
```

```python
import functools

import jax
import jax.numpy as jnp
import numpy as np
from jax import lax
from jax.experimental import pallas as pl
from jax.experimental.pallas import tpu as pltpu

F32 = jnp.float32
BF16 = jnp.bfloat16

LANES = 128
EPS = 1e-6
ROPE_BASE = 10000.0
RET_HEADS = 4
RET_DH = 128
RET_CHUNK = 128
CONV_K = 3
PEER_HEADS = 8
PEER_NKEYS = 128
PEER_HALF = 128
PEER_TOPK = 16
N_MOD = 6
VMEM_LIMIT = 56 * 1024 * 1024

NT = (((1,), (1,)), ((), ()))
TN = (((0,), (0,)), ((), ()))


def _cparams(sem):
    return pltpu.CompilerParams(dimension_semantics=sem, vmem_limit_bytes=VMEM_LIMIT)


def _rms(x, g):
    return x * lax.rsqrt(jnp.mean(x * x, axis=-1, keepdims=True) + EPS) * g


def _silu(x):
    return x / (1.0 + jnp.exp(-x))


def _rotary(x, cos, sin_signed):
    return x * cos + pltpu.roll(x, RET_DH // 2, axis=1) * sin_signed


def _mod_kernel(c_ref, w_ref, b_ref, o_ref):
    a = _silu(c_ref[...]).astype(BF16)
    o_ref[...] = jnp.dot(a, w_ref[...].astype(BF16), preferred_element_type=F32) + b_ref[...]


def _modulation(c, w_ada, b_ada):
    n, d = c.shape
    cols = w_ada.shape[1]
    blk = d
    return pl.pallas_call(
        _mod_kernel,
        grid=(cols // blk,),
        in_specs=[pl.BlockSpec((n, d), lambda j: (0, 0)),
                  pl.BlockSpec((d, blk), lambda j: (0, j)),
                  pl.BlockSpec((1, blk), lambda j: (0, j))],
        out_specs=pl.BlockSpec((n, blk), lambda j: (0, j)),
        out_shape=jax.ShapeDtypeStruct((n, cols), F32),
        compiler_params=_cparams(("parallel",)),
        name="mod",
    )(c, w_ada, b_ada.reshape(1, cols))


def _mix_kernel(x_ref, mod_ref, cos_ref, sin_ref, dec_ref, cdm_ref, kdm_ref, g128_ref,
                win_ref, wout_ref, gpre1_ref, gpost1_ref, gpre2_ref, cw_ref, cb_ref,
                x1_ref, h2_ref, sfin_ref, cache_ref,
                s_scr, ext_scr, mix_scr, *, tt):
    j = pl.program_id(1)
    rw = RET_HEADS * RET_DH
    cwid = cw_ref.shape[1]

    @pl.when(j == 0)
    def _():
        s_scr[...] = jnp.zeros_like(s_scr)
        ext_scr[0:8, :] = jnp.zeros((8, cwid), F32)

    x = x_ref[0]
    mod = mod_ref[0]
    sh1, sc1, gt1, sh2, sc2, gt2 = [mod[i:i + 1] for i in range(N_MOD)]
    h = (_rms(x, gpre1_ref[...]) * (1.0 + sc1) + sh1).astype(BF16)

    def proj(lo, width):
        return jnp.dot(h, win_ref[:, lo:lo + width], preferred_element_type=F32)

    pq = proj(0, rw)
    pk = proj(rw, rw)
    pv = proj(2 * rw, rw)
    pg = proj(3 * rw, rw)

    for c in range(tt // RET_CHUNK):
        r0 = c * RET_CHUNK
        cosb = cos_ref[r0:r0 + RET_CHUNK, :]
        sinb = sin_ref[r0:r0 + RET_CHUNK, :]
        for hh in range(RET_HEADS):
            l0 = hh * RET_DH
            q = _rotary(pq[r0:r0 + RET_CHUNK, l0:l0 + RET_DH], cosb, sinb)
            k = _rotary(pk[r0:r0 + RET_CHUNK, l0:l0 + RET_DH], cosb, sinb) * (RET_DH ** -0.5)
            v = pv[r0:r0 + RET_CHUNK, l0:l0 + RET_DH]
            g = pg[r0:r0 + RET_CHUNK, l0:l0 + RET_DH]
            qb, kb, vb = q.astype(BF16), k.astype(BF16), v.astype(BF16)
            s_old = s_scr[hh]
            scores = lax.dot_general(qb, kb, NT, preferred_element_type=F32) * dec_ref[hh]
            inner = jnp.dot(scores.astype(BF16), vb, preferred_element_type=F32)
            cross = jnp.dot(qb, s_old.astype(BF16), preferred_element_type=F32) * cdm_ref[hh]
            o = inner + cross
            kd = (k * kdm_ref[hh]).astype(BF16)
            s_scr[hh] = g128_ref[hh] * s_old + lax.dot_general(kd, vb, TN, preferred_element_type=F32)
            mu = jnp.mean(o, axis=-1, keepdims=True)
            oc = o - mu
            on = oc * lax.rsqrt(jnp.mean(oc * oc, axis=-1, keepdims=True) + EPS)
            mix_scr[r0:r0 + RET_CHUNK, l0:l0 + RET_DH] = (_silu(g) * on).astype(BF16)

    pb = proj(4 * rw, cwid)
    u = proj(4 * rw + cwid, cwid) * proj(4 * rw + 2 * cwid, cwid)
    ext_scr[8:8 + tt, :] = u
    y = (cb_ref[...] + cw_ref[0:1, :] * ext_scr[6:6 + tt, :]
         + cw_ref[1:2, :] * ext_scr[7:7 + tt, :] + cw_ref[2:3, :] * u)
    mix_scr[:, rw:rw + cwid] = (pb * y).astype(BF16)
    cache_ref[0] = ext_scr[tt + 6:tt + 8, :]
    ext_scr[0:8, :] = ext_scr[tt:tt + 8, :]

    m = jnp.dot(mix_scr[...], wout_ref[...], preferred_element_type=F32)
    x1 = x + gt1 * _rms(m, gpost1_ref[...])
    x1_ref[0] = x1
    h2_ref[0] = (_rms(x1, gpre2_ref[...]) * (1.0 + sc2) + sh2).astype(BF16)

    @pl.when(j == pl.num_programs(1) - 1)
    def _():
        sfin_ref[0] = s_scr[...]


def _decay_tables():
    lg = jnp.log(1.0 - 2.0 ** (-5.0 - jnp.arange(RET_HEADS, dtype=F32)))
    L = RET_CHUNK
    idx = jnp.arange(L, dtype=F32)
    rel = idx[:, None] - idx[None, :]
    dec = jnp.where(rel[None] >= 0, jnp.exp(lg[:, None, None] * jnp.maximum(rel, 0.0)[None]), 0.0)
    ones = jnp.ones((1, 1, RET_DH), F32)
    cdm = jnp.exp(lg[:, None] * (idx[None, :] + 1.0))[:, :, None] * ones
    kdm = jnp.exp(lg[:, None] * (L - 1.0 - idx[None, :]))[:, :, None] * ones
    g128 = jnp.exp(lg * L)[:, None, None] * jnp.ones((1, L, RET_DH), F32)
    return dec, cdm, kdm, g128


def _rope_tables(pos):
    half = RET_DH // 2
    freqs = ROPE_BASE ** (-jnp.arange(half, dtype=F32) / half)
    ang = pos.astype(F32)[:, None] * freqs[None, :]
    cos, sin = jnp.cos(ang), jnp.sin(ang)
    return jnp.concatenate([cos, cos], axis=-1), jnp.concatenate([-sin, sin], axis=-1)


def _prompt_mixer(x, mod, w_in, w_out, g_pre1, g_post1, g_pre2, conv_w, conv_b, tt):
    b, s, d = x.shape
    rw = RET_HEADS * RET_DH
    cwid = conv_w.shape[1]
    cos, sin = _rope_tables(jnp.arange(s, dtype=jnp.int32))
    dec, cdm, kdm, g128 = _decay_tables()
    tab = pl.BlockSpec((RET_HEADS, RET_CHUNK, RET_DH), lambda i, j: (0, 0, 0))
    row = pl.BlockSpec((1, d), lambda i, j: (0, 0))
    return pl.pallas_call(
        functools.partial(_mix_kernel, tt=tt),
        grid=(b, s // tt),
        in_specs=[pl.BlockSpec((1, tt, d), lambda i, j: (i, j, 0)),
                  pl.BlockSpec((1, N_MOD, d), lambda i, j: (i, 0, 0)),
                  pl.BlockSpec((tt, RET_DH), lambda i, j: (j, 0)),
                  pl.BlockSpec((tt, RET_DH), lambda i, j: (j, 0)),
                  tab, tab, tab, tab,
                  pl.BlockSpec(w_in.shape, lambda i, j: (0, 0)),
                  pl.BlockSpec(w_out.shape, lambda i, j: (0, 0)),
                  row, row, row,
                  pl.BlockSpec((CONV_K, cwid), lambda i, j: (0, 0)),
                  pl.BlockSpec((1, cwid), lambda i, j: (0, 0))],
        out_specs=[pl.BlockSpec((1, tt, d), lambda i, j: (i, j, 0)),
                   pl.BlockSpec((1, tt, d), lambda i, j: (i, j, 0)),
                   pl.BlockSpec((1, RET_HEADS, RET_DH, RET_DH), lambda i, j: (i, 0, 0, 0)),
                   pl.BlockSpec((1, CONV_K - 1, cwid), lambda i, j: (i, 0, 0))],
        out_shape=[jax.ShapeDtypeStruct((b, s, d), F32),
                   jax.ShapeDtypeStruct((b, s, d), BF16),
                   jax.ShapeDtypeStruct((b, RET_HEADS, RET_DH, RET_DH), F32),
                   jax.ShapeDtypeStruct((b, CONV_K - 1, cwid), F32)],
        scratch_shapes=[pltpu.VMEM((RET_HEADS, RET_DH, RET_DH), F32),
                        pltpu.VMEM((tt + 8, cwid), F32),
                        pltpu.VMEM((tt, rw + cwid), BF16)],
        compiler_params=_cparams(("parallel", "arbitrary")),
        name="mix",
    )(x, mod, cos, sin, dec, cdm, kdm, g128, w_in, w_out,
      g_pre1.reshape(1, d), g_post1.reshape(1, d), g_pre2.reshape(1, d),
      conv_w, conv_b.reshape(1, cwid))


SMP_BLOCK = 8


def _ret_gamma(hh):
    return 1.0 - 2.0 ** (-5.0 - hh)


def _smp_proj_kernel(x_ref, mod_ref, cos_ref, sin_ref, win_ref, gpre1_ref, cw_ref, cb_ref, cache_ref,
                     q_ref, k_ref, v_ref, g_ref, cout_ref, cache_out_ref):
    d = x_ref.shape[1]
    rw = RET_HEADS * RET_DH
    cwid = cw_ref.shape[1]
    x = x_ref[...]
    sh1 = mod_ref[:, 0:d]
    sc1 = mod_ref[:, d:2 * d]
    h = (_rms(x, gpre1_ref[...]) * (1.0 + sc1) + sh1).astype(BF16)
    p = jnp.dot(h, win_ref[...], preferred_element_type=F32)
    cosb, sinb = cos_ref[...], sin_ref[...]
    for hh in range(RET_HEADS):
        l0 = hh * RET_DH
        q_ref[:, l0:l0 + RET_DH] = _rotary(p[:, l0:l0 + RET_DH], cosb, sinb)
        k_ref[:, l0:l0 + RET_DH] = _rotary(p[:, rw + l0:rw + l0 + RET_DH], cosb, sinb) * (RET_DH ** -0.5)
    v_ref[...] = p[:, 2 * rw:3 * rw]
    g_ref[...] = p[:, 3 * rw:4 * rw]
    pb = p[:, 4 * rw:4 * rw + cwid]
    u = p[:, 4 * rw + cwid:4 * rw + 2 * cwid] * p[:, 4 * rw + 2 * cwid:4 * rw + 3 * cwid]
    buf0 = cache_ref[:, 0:cwid]
    buf1 = cache_ref[:, cwid:2 * cwid]
    y = cb_ref[...] + cw_ref[0:1, :] * buf0 + cw_ref[1:2, :] * buf1 + cw_ref[2:3, :] * u
    cout_ref[...] = pb * y
    cache_out_ref[:, 0:cwid] = buf1
    cache_out_ref[:, cwid:2 * cwid] = u


def _smp_state_kernel(q_ref, k_ref, v_ref, s0_ref, o_ref, snew_ref):
    nb = SMP_BLOCK
    row = lax.broadcasted_iota(jnp.int32, (nb, RET_DH), 0)
    zpad = jnp.zeros((RET_CHUNK - nb, RET_DH), F32)
    for hh in range(RET_HEADS):
        l0 = hh * RET_DH
        gam = _ret_gamma(hh)
        qb = q_ref[:, l0:l0 + RET_DH].astype(BF16)
        kb = k_ref[:, l0:l0 + RET_DH].astype(BF16)
        vb = v_ref[:, l0:l0 + RET_DH].astype(BF16)
        s_list = [s0_ref[n, hh] for n in range(nb)]
        scat = jnp.concatenate(s_list, axis=1).astype(BF16)
        res = jnp.dot(qb, scat, preferred_element_type=F32)
        cross = jnp.zeros((nb, RET_DH), F32)
        for n in range(nb):
            cross = cross + jnp.where(row == n, res[:, n * RET_DH:(n + 1) * RET_DH], 0.0)
        sc = jnp.sum(qb.astype(F32) * kb.astype(F32), axis=-1, keepdims=True)
        o_ref[:, l0:l0 + RET_DH] = sc.astype(BF16).astype(F32) * vb.astype(F32) + cross * gam
        kpad = jnp.concatenate([kb.astype(F32), zpad], axis=0).astype(BF16)
        vf = vb.astype(F32)
        w = jnp.concatenate([jnp.where(row == n, vf, 0.0) for n in range(nb)], axis=1)
        wpad = jnp.concatenate([w, jnp.zeros((RET_CHUNK - nb, nb * RET_DH), F32)], axis=0).astype(BF16)
        kv = lax.dot_general(kpad, wpad, TN, preferred_element_type=F32)
        for n in range(nb):
            snew_ref[n, hh] = gam * s_list[n] + kv[:, n * RET_DH:(n + 1) * RET_DH]


def _smp_post_kernel(x_ref, mod_ref, o_ref, g_ref, cout_ref, wout_ref, gpost1_ref, gpre2_ref,
                     x1_ref, h2_ref):
    d = x_ref.shape[1]
    rw = RET_HEADS * RET_DH
    x = x_ref[...]
    gt1 = mod_ref[:, 2 * d:3 * d]
    sh2 = mod_ref[:, 3 * d:4 * d]
    sc2 = mod_ref[:, 4 * d:5 * d]
    m = jnp.dot(cout_ref[...].astype(BF16), wout_ref[rw:, :], preferred_element_type=F32)
    for hh in range(RET_HEADS):
        l0 = hh * RET_DH
        o = o_ref[:, l0:l0 + RET_DH]
        oc = o - jnp.mean(o, axis=-1, keepdims=True)
        on = oc * lax.rsqrt(jnp.mean(oc * oc, axis=-1, keepdims=True) + EPS)
        r = (_silu(g_ref[:, l0:l0 + RET_DH]) * on).astype(BF16)
        m = m + jnp.dot(r, wout_ref[l0:l0 + RET_DH, :], preferred_element_type=F32)
    x1 = x + gt1 * _rms(m, gpost1_ref[...])
    x1_ref[...] = x1
    h2_ref[...] = (_rms(x1, gpre2_ref[...]) * (1.0 + sc2) + sh2).astype(BF16)


def _sample_mixer(x, mod, state, cache, pos, w_in, w_out, g_pre1, g_post1, g_pre2, conv_w, conv_b):
    n, d = x.shape
    rw = RET_HEADS * RET_DH
    cwid = conv_w.shape[1]
    cos, sin = _rope_tables(pos)
    f = jax.ShapeDtypeStruct
    q, k, v, g, cout, cache_out = pl.pallas_call(
        _smp_proj_kernel,
        out_shape=[f((n, rw), F32)] * 4 + [f((n, cwid), F32), f((n, (CONV_K - 1) * cwid), F32)],
        compiler_params=pltpu.CompilerParams(vmem_limit_bytes=VMEM_LIMIT),
        name="smp_proj",
    )(x, mod, cos, sin, w_in, g_pre1.reshape(1, d), conv_w, conv_b.reshape(1, cwid),
      cache.reshape(n, (CONV_K - 1) * cwid))
    nb = SMP_BLOCK
    vec = pl.BlockSpec((nb, rw), lambda i: (i, 0))
    st = pl.BlockSpec((nb, RET_HEADS, RET_DH, RET_DH), lambda i: (i, 0, 0, 0))
    o, snew = pl.pallas_call(
        _smp_state_kernel,
        grid=(n // nb,),
        in_specs=[vec, vec, vec, st],
        out_specs=[vec, st],
        out_shape=[f((n, rw), F32), f(state.shape, F32)],
        compiler_params=_cparams(("parallel",)),
        name="smp_state",
    )(q, k, v, state)
    x1, h2 = pl.pallas_call(
        _smp_post_kernel,
        out_shape=[f((n, d), F32), f((n, d), BF16)],
        compiler_params=pltpu.CompilerParams(vmem_limit_bytes=VMEM_LIMIT),
        name="smp_post",
    )(x, mod, o, g, cout, w_out, g_post1.reshape(1, d), g_pre2.reshape(1, d))
    return x1, h2, snew, cache_out.reshape(n, CONV_K - 1, cwid)


_CAND = [(a, b) for a in range(PEER_TOPK) for b in range(PEER_TOPK) if (a + 1) * (b + 1) <= PEER_TOPK]


def _oddeven_mergesort_network(n):
    comps = []
    p = 1
    while p < n:
        k = p
        while k >= 1:
            for j in range(k % p, n - k, 2 * k):
                for i in range(min(k, n - j - k)):
                    if (i + j) // (2 * p) == (i + j + k) // (2 * p):
                        comps.append((i + j, i + j + k))
            k //= 2
        p *= 2
    return comps


_SORT16 = _oddeven_mergesort_network(PEER_TOPK)


def _sort16(x):
    x = list(x)
    for i, j in _SORT16:
        x[i], x[j] = jnp.maximum(x[i], x[j]), jnp.minimum(x[i], x[j])
    return x


def _merge_top16(x, y):
    x = [jnp.maximum(x[k], y[PEER_TOPK - 1 - k]) for k in range(PEER_TOPK)]
    d = PEER_TOPK // 2
    while d >= 1:
        for i in range(PEER_TOPK):
            if i & d == 0:
                x[i], x[i + d] = jnp.maximum(x[i], x[i + d]), jnp.minimum(x[i], x[i + d])
        d //= 2
    return x


def _sorted_top16(x):
    x = _sort16(x)
    for shift in (4, 2, 1):
        x = _merge_top16(x, [pltpu.roll(v, shift, axis=0) for v in x])
    return x


def _sel_kernel(h2_ref, wq_ref, keys_ref, e1_ref, cnt_ref, e2_ref, r2_ref,
                s1_scr, s2_scr, a_scr, b_scr, *, tt):
    ntc = tt // LANES
    neg = -jnp.inf
    q = jnp.dot(h2_ref[...], wq_ref[...], preferred_element_type=F32).astype(BF16)
    for h in range(PEER_HEADS):
        for c, scr in ((0, s1_scr), (1, s2_scr)):
            col = (h * 2 + c) * PEER_HALF
            scr[h] = lax.dot_general(keys_ref[h, c], q[:, col:col + PEER_HALF], NT,
                                     preferred_element_type=F32)

    def token_chunk(tc, carry):
        lanes = pl.ds(pl.multiple_of(tc * LANES, LANES), LANES)
        for h in range(PEER_HEADS):
            for side, (s_scr, v_scr) in enumerate(((s1_scr, a_scr), (s2_scr, b_scr))):
                s = s_scr[h, :, lanes]
                top = _sorted_top16([s[v * 8:(v + 1) * 8] for v in range(PEER_NKEYS // 8)])
                for k in range(PEER_TOPK):
                    v_scr[k, h:h + 1, lanes] = top[k][0:1]
                if side == 1:
                    rank = jnp.zeros(s.shape, F32)
                    for k in range(PEER_TOPK):
                        rank = jnp.where(top[k][0:1] > s, float(k + 1), rank)
                    r2_ref[h, :, lanes] = rank.astype(BF16)
        av = [a_scr[k, :, lanes] for k in range(PEER_TOPK)]
        bv = [b_scr[k, :, lanes] for k in range(PEER_TOPK)]
        cands = [av[a] + bv[b] for a, b in _CAND]
        pad = [jnp.full(cands[0].shape, neg, F32)] * (-len(cands) % PEER_TOPK)
        groups = [_sort16((cands + pad)[g:g + PEER_TOPK]) for g in range(0, len(cands) + len(pad), PEER_TOPK)]
        while len(groups) > 1:
            groups = [_merge_top16(groups[g], groups[g + 1]) for g in range(0, len(groups), 2)]
        t = groups[0][PEER_TOPK - 1]
        z = None
        for c in cands:
            term = jnp.where(c >= t, jnp.exp(c - cands[0]), 0.0)
            z = term if z is None else z + term
        zi = 1.0 / z
        cntr = []
        for a in range(PEER_TOPK):
            c = jnp.zeros(t.shape, F32)
            for b in range(PEER_TOPK):
                c = jnp.where(av[a] + bv[b] >= t, float(b + 1), c)
            cntr.append(c)
        for h in range(PEER_HEADS):
            s1 = s1_scr[h, :, lanes]
            s2 = s2_scr[h, :, lanes]
            cnt = jnp.zeros(s1.shape, F32)
            for k in reversed(range(PEER_TOPK)):
                cnt = jnp.where(s1 >= av[k][h:h + 1], cntr[k][h:h + 1], cnt)
            cnt_ref[h, :, lanes] = cnt
            e1_ref[h, :, lanes] = jnp.exp(s1 - av[0][h:h + 1]) * zi[h:h + 1]
            e2_ref[h, :, lanes] = jnp.exp(s2 - bv[0][h:h + 1]).astype(BF16)
        return carry

    lax.fori_loop(0, ntc, token_chunk, 0)


def _peer_select(h2, w_q, keys, tt):
    t, d = h2.shape
    shape = (PEER_HEADS, PEER_NKEYS, t)
    out = [jax.ShapeDtypeStruct(shape, F32)] * 2 + [jax.ShapeDtypeStruct(shape, BF16)] * 2
    ospec = pl.BlockSpec((PEER_HEADS, PEER_NKEYS, tt), lambda i: (0, 0, i))
    return pl.pallas_call(
        functools.partial(_sel_kernel, tt=tt),
        grid=(t // tt,),
        in_specs=[pl.BlockSpec((tt, d), lambda i: (i, 0)),
                  pl.BlockSpec(w_q.shape, lambda i: (0, 0)),
                  pl.BlockSpec(keys.shape, lambda i: (0, 0, 0, 0))],
        out_specs=[ospec] * 4,
        out_shape=out,
        scratch_shapes=[pltpu.VMEM((PEER_HEADS, PEER_NKEYS, tt), F32),
                        pltpu.VMEM((PEER_HEADS, PEER_NKEYS, tt), F32),
                        pltpu.VMEM((PEER_TOPK, PEER_HEADS, tt), F32),
                        pltpu.VMEM((PEER_TOPK, PEER_HEADS, tt), F32)],
        compiler_params=_cparams(("parallel",)),
        name="sel",
    )(h2, w_q, keys)


PEER_ECHUNK = 512
PEER_JSUB = 16
PEER_IGROUP = 2


def _gelu_tanh_bf16(x):
    c = np.sqrt(2.0 / np.pi)
    c0 = jnp.full(x.shape, c, F32).astype(x.dtype)
    c1 = jnp.full(x.shape, 0.044715 * c, F32).astype(x.dtype)
    t = jnp.tanh(x * (c0 + c1 * (x * x)))
    return x * (0.5 + 0.5 * t)


def _peer_kernel(h2_ref, u_ref, vt_ref, e1_ref, cnt_ref, e2_ref, r2_ref, x1_ref, gt2_ref, gpost2_ref,
                 y_ref, acc_ref, c_scr, er_scr, xt_scr, *, eb):
    e = pl.program_id(1)
    tt = h2_ref.shape[0]
    ipc = PEER_ECHUNK // PEER_NKEYS
    r2_off = PEER_HEADS * PEER_NKEYS + PEER_JSUB

    @pl.when(e == 0)
    def _():
        acc_ref[...] = jnp.zeros_like(acc_ref)
        xt_scr[...] = h2_ref[...].T
        for h in range(PEER_HEADS):
            er_scr[h * PEER_NKEYS:(h + 1) * PEER_NKEYS, 0:tt] = e2_ref[h].astype(F32)
            er_scr[r2_off + h * PEER_NKEYS:r2_off + (h + 1) * PEER_NKEYS, 0:tt] = r2_ref[h].astype(F32)

    bshape = (PEER_JSUB, LANES)
    part = None
    for c in range(eb // PEER_ECHUNK):
        c0 = c * PEER_ECHUNK
        a = jnp.dot(u_ref[c0:c0 + PEER_ECHUNK, :], xt_scr[...], preferred_element_type=F32)
        for lc in range(tt // LANES):
            lanes = slice(lc * LANES, (lc + 1) * LANES)
            for ig in range(ipc // PEER_IGROUP):
                ils = [ig * PEER_IGROUP + t for t in range(PEER_IGROUP)]
                e1b, cntb = [], []
                for h in range(PEER_HEADS):
                    rows = [c * ipc + il for il in ils]
                    e1b.append([jnp.broadcast_to(e1_ref[h, i:i + 1, lanes], bshape).astype(BF16) for i in rows])
                    cntb.append([jnp.broadcast_to(cnt_ref[h, i:i + 1, lanes], bshape).astype(BF16) for i in rows])
                for jc in range(PEER_NKEYS // PEER_JSUB):
                    g = [None] * PEER_IGROUP
                    for h in range(PEER_HEADS):
                        j0 = h * PEER_NKEYS + jc * PEER_JSUB
                        r2v = er_scr[r2_off + j0:r2_off + j0 + PEER_JSUB, lanes].astype(BF16)
                        e2v = er_scr[j0:j0 + PEER_JSUB, lanes].astype(BF16)
                        for t in range(PEER_IGROUP):
                            term = jnp.where(r2v < cntb[h][t], e2v, 0.0) * e1b[h][t]
                            g[t] = term if g[t] is None else g[t] + term
                    for t, il in enumerate(ils):
                        r0 = il * PEER_NKEYS + jc * PEER_JSUB
                        act = _gelu_tanh_bf16(a[r0:r0 + PEER_JSUB, lanes].astype(BF16))
                        c_scr[c0 + r0:c0 + r0 + PEER_JSUB, lanes] = act * g[t]
        d2 = jnp.dot(vt_ref[:, c0:c0 + PEER_ECHUNK], c_scr[c0:c0 + PEER_ECHUNK, :],
                     preferred_element_type=F32)
        part = d2 if part is None else part + d2
    acc_ref[...] += part

    @pl.when(e == pl.num_programs(1) - 1)
    def _():
        f = acc_ref[...].T
        gt2 = gt2_ref[0] if len(gt2_ref.shape) == 3 else gt2_ref[...]
        y_ref[...] = x1_ref[...] + gt2 * _rms(f, gpost2_ref[...])


def _peer_dense(h2, x1, gt2, ub, vt, sel, g_post2, tt, eb, seq_len):
    t, d = h2.shape
    ne = ub.shape[0]
    e1, cnt, e2, r2 = sel
    ib = eb // PEER_NKEYS
    per_e = pl.BlockSpec((PEER_HEADS, ib, tt), lambda i, e: (0, e, i))
    per_t = pl.BlockSpec((PEER_HEADS, PEER_NKEYS, tt), lambda i, e: (0, 0, i))
    if seq_len == 1:
        gspec = pl.BlockSpec((tt, d), lambda i, e: (i, 0))
    else:
        assert seq_len % tt == 0
        tiles_per_seq = seq_len // tt
        gspec = pl.BlockSpec((1, 1, d), lambda i, e: (i // tiles_per_seq, 0, 0))
        gt2 = gt2.reshape(gt2.shape[0], 1, d)
    return pl.pallas_call(
        functools.partial(_peer_kernel, eb=eb),
        grid=(t // tt, ne // eb),
        in_specs=[pl.BlockSpec((tt, d), lambda i, e: (i, 0)),
                  pl.BlockSpec((eb, d), lambda i, e: (e, 0)),
                  pl.BlockSpec((d, eb), lambda i, e: (0, e)),
                  per_e, per_e, per_t, per_t,
                  pl.BlockSpec((tt, d), lambda i, e: (i, 0)),
                  gspec,
                  pl.BlockSpec((1, d), lambda i, e: (0, 0))],
        out_specs=pl.BlockSpec((tt, d), lambda i, e: (i, 0)),
        out_shape=jax.ShapeDtypeStruct((t, d), F32),
        scratch_shapes=[pltpu.VMEM((d, tt), F32), pltpu.VMEM((eb, tt), BF16),
                        pltpu.VMEM((2 * PEER_HEADS * PEER_NKEYS + PEER_JSUB, tt + LANES), F32),
                        pltpu.VMEM((d, tt), BF16)],
        compiler_params=_cparams(("parallel", "arbitrary")),
        name="peer",
    )(h2, ub, vt, e1, cnt, e2, r2, x1, gt2, g_post2.reshape(1, d))


def _prep_kernel(u_ref, v_ref, ub_ref, vt_ref):
    ub_ref[...] = u_ref[...].astype(BF16)
    vt_ref[...] = v_ref[...].T.astype(BF16)


def _prep_tables(u_tab, v_tab, blk=512):
    ne, d = u_tab.shape
    return pl.pallas_call(
        _prep_kernel,
        grid=(ne // blk,),
        in_specs=[pl.BlockSpec((blk, d), lambda i: (i, 0)), pl.BlockSpec((blk, d), lambda i: (i, 0))],
        out_specs=[pl.BlockSpec((blk, d), lambda i: (i, 0)), pl.BlockSpec((d, blk), lambda i: (0, i))],
        out_shape=[jax.ShapeDtypeStruct((ne, d), BF16), jax.ShapeDtypeStruct((d, ne), BF16)],
        compiler_params=_cparams(("parallel",)),
        name="prep",
    )(u_tab, v_tab)


PAST_LEN = 16384
MIX_TILE = 512
SEL_TILE = 512
PEER_TILE = 512
PEER_EBLOCK = 2048


def kernel(x_prompt, x_sample, c_prompt, c_sample, state_ret, cache_conv, w_ada, b_ada, g_pre1, g_post1, g_pre2, g_post2, w_in, w_out, conv_w, conv_b, w_q, sub_keys, u_tab, v_tab):
    depth = w_in.shape[0]
    assert depth == 1, "single-layer stack"
    l = 0
    b, s, d = x_prompt.shape
    n = x_sample.shape[0]
    assert x_sample.shape[1] == 1
    mod = _modulation(jnp.concatenate([c_prompt, c_sample], axis=0), w_ada[l], b_ada[l])
    mod_p, mod_s = mod[:b], mod[b:]
    w_in_b, w_out_b = w_in[l].astype(BF16), w_out[l].astype(BF16)
    w_q_b, keys_b = w_q[l].astype(BF16), sub_keys[l].astype(BF16)
    ub, vt = _prep_tables(u_tab[l], v_tab[l])

    x1p, h2p, sfin_p, cache_p = _prompt_mixer(
        x_prompt, mod_p.reshape(b, N_MOD, d), w_in_b, w_out_b,
        g_pre1[l], g_post1[l], g_pre2[l], conv_w[l], conv_b[l], min(MIX_TILE, s))
    x1s, h2s, sfin_s, cache_s = _sample_mixer(
        x_sample.reshape(n, d), mod_s, state_ret[l], cache_conv[l],
        PAST_LEN + jnp.arange(1, dtype=jnp.int32), w_in_b, w_out_b,
        g_pre1[l], g_post1[l], g_pre2[l], conv_w[l], conv_b[l])

    h2p = h2p.reshape(b * s, d)
    sel_p = _peer_select(h2p, w_q_b, keys_b, min(SEL_TILE, s))
    yp = _peer_dense(h2p, x1p.reshape(b * s, d), mod_p[:, 5 * d:6 * d], ub, vt, sel_p, g_post2[l],
                     min(PEER_TILE, s), PEER_EBLOCK, s)
    sel_s = _peer_select(h2s, w_q_b, keys_b, n)
    ys = _peer_dense(h2s, x1s, mod_s[:, 5 * d:6 * d], ub, vt, sel_s, g_post2[l], n, PEER_EBLOCK, 1)

    return (yp.reshape(b, s, d), ys.reshape(n, 1, d), sfin_p[None], cache_p[None],
            sfin_s[None], cache_s[None])
```

```python
import functools

import jax
import jax.numpy as jnp
import numpy as np
from jax import lax
from jax.experimental import pallas as pl
from jax.experimental.pallas import tpu as pltpu

F32 = jnp.float32
BF16 = jnp.bfloat16

LANES = 128
EPS = 1e-6
ROPE_BASE = 10000.0
RET_HEADS = 4
RET_DH = 128
RET_CHUNK = 128
CONV_K = 3
PEER_HEADS = 8
PEER_NKEYS = 128
PEER_HALF = 128
PEER_TOPK = 16
N_MOD = 6
VMEM_LIMIT = 56 * 1024 * 1024

NT = (((1,), (1,)), ((), ()))
TN = (((0,), (0,)), ((), ()))


def _cparams(sem):
    return pltpu.CompilerParams(dimension_semantics=sem, vmem_limit_bytes=VMEM_LIMIT)


def _rms(x, g):
    return x * lax.rsqrt(jnp.mean(x * x, axis=-1, keepdims=True) + EPS) * g


def _silu(x):
    return x / (1.0 + jnp.exp(-x))


def _rotary(x, cos, sin_signed):
    return x * cos + pltpu.roll(x, RET_DH // 2, axis=1) * sin_signed


def _mod_kernel(c_ref, w_ref, b_ref, o_ref):
    a = _silu(c_ref[...]).astype(BF16)
    o_ref[...] = jnp.dot(a, w_ref[...].astype(BF16), preferred_element_type=F32) + b_ref[...]


def _modulation(c, w_ada, b_ada):
    n, d = c.shape
    cols = w_ada.shape[1]
    blk = d
    return pl.pallas_call(
        _mod_kernel,
        grid=(cols // blk,),
        in_specs=[pl.BlockSpec((n, d), lambda j: (0, 0)),
                  pl.BlockSpec((d, blk), lambda j: (0, j)),
                  pl.BlockSpec((1, blk), lambda j: (0, j))],
        out_specs=pl.BlockSpec((n, blk), lambda j: (0, j)),
        out_shape=jax.ShapeDtypeStruct((n, cols), F32),
        compiler_params=_cparams(("parallel",)),
        name="mod",
    )(c, w_ada, b_ada.reshape(1, cols))


def _mix_kernel(x_ref, mod_ref, cos_ref, sin_ref, dec_ref, cdm_ref, kdm_ref, g128_ref,
                win_ref, wout_ref, gpre1_ref, gpost1_ref, gpre2_ref, cw_ref, cb_ref,
                x1_ref, h2_ref, sfin_ref, cache_ref,
                s_scr, ext_scr, mix_scr, *, tt):
    j = pl.program_id(1)
    rw = RET_HEADS * RET_DH
    cwid = cw_ref.shape[1]

    @pl.when(j == 0)
    def _():
        s_scr[...] = jnp.zeros_like(s_scr)
        ext_scr[0:8, :] = jnp.zeros((8, cwid), F32)

    x = x_ref[0]
    mod = mod_ref[0]
    sh1, sc1, gt1, sh2, sc2, gt2 = [mod[i:i + 1] for i in range(N_MOD)]
    h = (_rms(x, gpre1_ref[...]) * (1.0 + sc1) + sh1).astype(BF16)

    def proj(lo, width):
        return jnp.dot(h, win_ref[:, lo:lo + width], preferred_element_type=F32)

    pq = proj(0, rw)
    pk = proj(rw, rw)
    pv = proj(2 * rw, rw)
    pg = proj(3 * rw, rw)

    for c in range(tt // RET_CHUNK):
        r0 = c * RET_CHUNK
        cosb = cos_ref[r0:r0 + RET_CHUNK, :]
        sinb = sin_ref[r0:r0 + RET_CHUNK, :]
        for hh in range(RET_HEADS):
            l0 = hh * RET_DH
            q = _rotary(pq[r0:r0 + RET_CHUNK, l0:l0 + RET_DH], cosb, sinb)
            k = _rotary(pk[r0:r0 + RET_CHUNK, l0:l0 + RET_DH], cosb, sinb) * (RET_DH ** -0.5)
            v = pv[r0:r0 + RET_CHUNK, l0:l0 + RET_DH]
            g = pg[r0:r0 + RET_CHUNK, l0:l0 + RET_DH]
            qb, kb, vb = q.astype(BF16), k.astype(BF16), v.astype(BF16)
            s_old = s_scr[hh]
            scores = lax.dot_general(qb, kb, NT, preferred_element_type=F32) * dec_ref[hh]
            inner = jnp.dot(scores.astype(BF16), vb, preferred_element_type=F32)
            cross = jnp.dot(qb, s_old.astype(BF16), preferred_element_type=F32) * cdm_ref[hh]
            o = inner + cross
            kd = (k * kdm_ref[hh]).astype(BF16)
            s_scr[hh] = g128_ref[hh] * s_old + lax.dot_general(kd, vb, TN, preferred_element_type=F32)
            mu = jnp.mean(o, axis=-1, keepdims=True)
            oc = o - mu
            on = oc * lax.rsqrt(jnp.mean(oc * oc, axis=-1, keepdims=True) + EPS)
            mix_scr[r0:r0 + RET_CHUNK, l0:l0 + RET_DH] = (_silu(g) * on).astype(BF16)

    pb = proj(4 * rw, cwid)
    u = proj(4 * rw + cwid, cwid) * proj(4 * rw + 2 * cwid, cwid)
    ext_scr[8:8 + tt, :] = u
    y = (cb_ref[...] + cw_ref[0:1, :] * ext_scr[6:6 + tt, :]
         + cw_ref[1:2, :] * ext_scr[7:7 + tt, :] + cw_ref[2:3, :] * u)
    mix_scr[:, rw:rw + cwid] = (pb * y).astype(BF16)
    cache_ref[0] = ext_scr[tt + 6:tt + 8, :]
    ext_scr[0:8, :] = ext_scr[tt:tt + 8, :]

    m = jnp.dot(mix_scr[...], wout_ref[...], preferred_element_type=F32)
    x1 = x + gt1 * _rms(m, gpost1_ref[...])
    x1_ref[0] = x1
    h2_ref[0] = (_rms(x1, gpre2_ref[...]) * (1.0 + sc2) + sh2).astype(BF16)

    @pl.when(j == pl.num_programs(1) - 1)
    def _():
        sfin_ref[0] = s_scr[...]


def _decay_tables():
    lg = jnp.log(1.0 - 2.0 ** (-5.0 - jnp.arange(RET_HEADS, dtype=F32)))
    L = RET_CHUNK
    idx = jnp.arange(L, dtype=F32)
    rel = idx[:, None] - idx[None, :]
    dec = jnp.where(rel[None] >= 0, jnp.exp(lg[:, None, None] * jnp.maximum(rel, 0.0)[None]), 0.0)
    ones = jnp.ones((1, 1, RET_DH), F32)
    cdm = jnp.exp(lg[:, None] * (idx[None, :] + 1.0))[:, :, None] * ones
    kdm = jnp.exp(lg[:, None] * (L - 1.0 - idx[None, :]))[:, :, None] * ones
    g128 = jnp.exp(lg * L)[:, None, None] * jnp.ones((1, L, RET_DH), F32)
    return dec, cdm, kdm, g128


def _rope_tables(pos):
    half = RET_DH // 2
    freqs = ROPE_BASE ** (-jnp.arange(half, dtype=F32) / half)
    ang = pos.astype(F32)[:, None] * freqs[None, :]
    cos, sin = jnp.cos(ang), jnp.sin(ang)
    return jnp.concatenate([cos, cos], axis=-1), jnp.concatenate([-sin, sin], axis=-1)


def _prompt_mixer(x, mod, w_in, w_out, g_pre1, g_post1, g_pre2, conv_w, conv_b, tt):
    b, s, d = x.shape
    rw = RET_HEADS * RET_DH
    cwid = conv_w.shape[1]
    cos, sin = _rope_tables(jnp.arange(s, dtype=jnp.int32))
    dec, cdm, kdm, g128 = _decay_tables()
    tab = pl.BlockSpec((RET_HEADS, RET_CHUNK, RET_DH), lambda i, j: (0, 0, 0))
    row = pl.BlockSpec((1, d), lambda i, j: (0, 0))
    return pl.pallas_call(
        functools.partial(_mix_kernel, tt=tt),
        grid=(b, s // tt),
        in_specs=[pl.BlockSpec((1, tt, d), lambda i, j: (i, j, 0)),
                  pl.BlockSpec((1, N_MOD, d), lambda i, j: (i, 0, 0)),
                  pl.BlockSpec((tt, RET_DH), lambda i, j: (j, 0)),
                  pl.BlockSpec((tt, RET_DH), lambda i, j: (j, 0)),
                  tab, tab, tab, tab,
                  pl.BlockSpec(w_in.shape, lambda i, j: (0, 0)),
                  pl.BlockSpec(w_out.shape, lambda i, j: (0, 0)),
                  row, row, row,
                  pl.BlockSpec((CONV_K, cwid), lambda i, j: (0, 0)),
                  pl.BlockSpec((1, cwid), lambda i, j: (0, 0))],
        out_specs=[pl.BlockSpec((1, tt, d), lambda i, j: (i, j, 0)),
                   pl.BlockSpec((1, tt, d), lambda i, j: (i, j, 0)),
                   pl.BlockSpec((1, RET_HEADS, RET_DH, RET_DH), lambda i, j: (i, 0, 0, 0)),
                   pl.BlockSpec((1, CONV_K - 1, cwid), lambda i, j: (i, 0, 0))],
        out_shape=[jax.ShapeDtypeStruct((b, s, d), F32),
                   jax.ShapeDtypeStruct((b, s, d), BF16),
                   jax.ShapeDtypeStruct((b, RET_HEADS, RET_DH, RET_DH), F32),
                   jax.ShapeDtypeStruct((b, CONV_K - 1, cwid), F32)],
        scratch_shapes=[pltpu.VMEM((RET_HEADS, RET_DH, RET_DH), F32),
                        pltpu.VMEM((tt + 8, cwid), F32),
                        pltpu.VMEM((tt, rw + cwid), BF16)],
        compiler_params=_cparams(("parallel", "arbitrary")),
        name="mix",
    )(x, mod, cos, sin, dec, cdm, kdm, g128, w_in, w_out,
      g_pre1.reshape(1, d), g_post1.reshape(1, d), g_pre2.reshape(1, d),
      conv_w, conv_b.reshape(1, cwid))


SMP_BLOCK = 8


def _ret_gamma(hh):
    return 1.0 - 2.0 ** (-5.0 - hh)


def _smp_proj_kernel(x_ref, mod_ref, cos_ref, sin_ref, win_ref, gpre1_ref, cw_ref, cb_ref, cache_ref,
                     q_ref, k_ref, v_ref, g_ref, cout_ref, cache_out_ref):
    d = x_ref.shape[1]
    rw = RET_HEADS * RET_DH
    cwid = cw_ref.shape[1]
    x = x_ref[...]
    sh1 = mod_ref[:, 0:d]
    sc1 = mod_ref[:, d:2 * d]
    h = (_rms(x, gpre1_ref[...]) * (1.0 + sc1) + sh1).astype(BF16)
    p = jnp.dot(h, win_ref[...], preferred_element_type=F32)
    cosb, sinb = cos_ref[...], sin_ref[...]
    for hh in range(RET_HEADS):
        l0 = hh * RET_DH
        q_ref[:, l0:l0 + RET_DH] = _rotary(p[:, l0:l0 + RET_DH], cosb, sinb)
        k_ref[:, l0:l0 + RET_DH] = _rotary(p[:, rw + l0:rw + l0 + RET_DH], cosb, sinb) * (RET_DH ** -0.5)
    v_ref[...] = p[:, 2 * rw:3 * rw]
    g_ref[...] = p[:, 3 * rw:4 * rw]
    pb = p[:, 4 * rw:4 * rw + cwid]
    u = p[:, 4 * rw + cwid:4 * rw + 2 * cwid] * p[:, 4 * rw + 2 * cwid:4 * rw + 3 * cwid]
    buf0 = cache_ref[:, 0:cwid]
    buf1 = cache_ref[:, cwid:2 * cwid]
    y = cb_ref[...] + cw_ref[0:1, :] * buf0 + cw_ref[1:2, :] * buf1 + cw_ref[2:3, :] * u
    cout_ref[...] = pb * y
    cache_out_ref[:, 0:cwid] = buf1
    cache_out_ref[:, cwid:2 * cwid] = u


def _smp_state_kernel(q_ref, k_ref, v_ref, s0_ref, o_ref, snew_ref):
    nb = SMP_BLOCK
    row = lax.broadcasted_iota(jnp.int32, (nb, RET_DH), 0)
    zpad = jnp.zeros((RET_CHUNK - nb, RET_DH), F32)
    for hh in range(RET_HEADS):
        l0 = hh * RET_DH
        gam = _ret_gamma(hh)
        qb = q_ref[:, l0:l0 + RET_DH].astype(BF16)
        kb = k_ref[:, l0:l0 + RET_DH].astype(BF16)
        vb = v_ref[:, l0:l0 + RET_DH].astype(BF16)
        s_list = [s0_ref[n, hh] for n in range(nb)]
        scat = jnp.concatenate(s_list, axis=1).astype(BF16)
        res = jnp.dot(qb, scat, preferred_element_type=F32)
        cross = jnp.zeros((nb, RET_DH), F32)
        for n in range(nb):
            cross = cross + jnp.where(row == n, res[:, n * RET_DH:(n + 1) * RET_DH], 0.0)
        sc = jnp.sum(qb.astype(F32) * kb.astype(F32), axis=-1, keepdims=True)
        o_ref[:, l0:l0 + RET_DH] = sc.astype(BF16).astype(F32) * vb.astype(F32) + cross * gam
        kpad = jnp.concatenate([kb.astype(F32), zpad], axis=0).astype(BF16)
        vf = vb.astype(F32)
        w = jnp.concatenate([jnp.where(row == n, vf, 0.0) for n in range(nb)], axis=1)
        wpad = jnp.concatenate([w, jnp.zeros((RET_CHUNK - nb, nb * RET_DH), F32)], axis=0).astype(BF16)
        kv = lax.dot_general(kpad, wpad, TN, preferred_element_type=F32)
        for n in range(nb):
            snew_ref[n, hh] = gam * s_list[n] + kv[:, n * RET_DH:(n + 1) * RET_DH]


def _smp_post_kernel(x_ref, mod_ref, o_ref, g_ref, cout_ref, wout_ref, gpost1_ref, gpre2_ref,
                     x1_ref, h2_ref):
    d = x_ref.shape[1]
    rw = RET_HEADS * RET_DH
    x = x_ref[...]
    gt1 = mod_ref[:, 2 * d:3 * d]
    sh2 = mod_ref[:, 3 * d:4 * d]
    sc2 = mod_ref[:, 4 * d:5 * d]
    m = jnp.dot(cout_ref[...].astype(BF16), wout_ref[rw:, :], preferred_element_type=F32)
    for hh in range(RET_HEADS):
        l0 = hh * RET_DH
        o = o_ref[:, l0:l0 + RET_DH]
        oc = o - jnp.mean(o, axis=-1, keepdims=True)
        on = oc * lax.rsqrt(jnp.mean(oc * oc, axis=-1, keepdims=True) + EPS)
        r = (_silu(g_ref[:, l0:l0 + RET_DH]) * on).astype(BF16)
        m = m + jnp.dot(r, wout_ref[l0:l0 + RET_DH, :], preferred_element_type=F32)
    x1 = x + gt1 * _rms(m, gpost1_ref[...])
    x1_ref[...] = x1
    h2_ref[...] = (_rms(x1, gpre2_ref[...]) * (1.0 + sc2) + sh2).astype(BF16)


def _sample_mixer(x, mod, state, cache, pos, w_in, w_out, g_pre1, g_post1, g_pre2, conv_w, conv_b):
    n, d = x.shape
    rw = RET_HEADS * RET_DH
    cwid = conv_w.shape[1]
    cos, sin = _rope_tables(pos)
    f = jax.ShapeDtypeStruct
    q, k, v, g, cout, cache_out = pl.pallas_call(
        _smp_proj_kernel,
        out_shape=[f((n, rw), F32)] * 4 + [f((n, cwid), F32), f((n, (CONV_K - 1) * cwid), F32)],
        compiler_params=pltpu.CompilerParams(vmem_limit_bytes=VMEM_LIMIT),
        name="smp_proj",
    )(x, mod, cos, sin, w_in, g_pre1.reshape(1, d), conv_w, conv_b.reshape(1, cwid),
      cache.reshape(n, (CONV_K - 1) * cwid))
    nb = SMP_BLOCK
    vec = pl.BlockSpec((nb, rw), lambda i: (i, 0))
    st = pl.BlockSpec((nb, RET_HEADS, RET_DH, RET_DH), lambda i: (i, 0, 0, 0))
    o, snew = pl.pallas_call(
        _smp_state_kernel,
        grid=(n // nb,),
        in_specs=[vec, vec, vec, st],
        out_specs=[vec, st],
        out_shape=[f((n, rw), F32), f(state.shape, F32)],
        compiler_params=_cparams(("parallel",)),
        name="smp_state",
    )(q, k, v, state)
    x1, h2 = pl.pallas_call(
        _smp_post_kernel,
        out_shape=[f((n, d), F32), f((n, d), BF16)],
        compiler_params=pltpu.CompilerParams(vmem_limit_bytes=VMEM_LIMIT),
        name="smp_post",
    )(x, mod, o, g, cout, w_out, g_post1.reshape(1, d), g_pre2.reshape(1, d))
    return x1, h2, snew, cache_out.reshape(n, CONV_K - 1, cwid)


_CAND = [(a, b) for a in range(PEER_TOPK) for b in range(PEER_TOPK) if (a + 1) * (b + 1) <= PEER_TOPK]


def _oddeven_mergesort_network(n):
    comps = []
    p = 1
    while p < n:
        k = p
        while k >= 1:
            for j in range(k % p, n - k, 2 * k):
                for i in range(min(k, n - j - k)):
                    if (i + j) // (2 * p) == (i + j + k) // (2 * p):
                        comps.append((i + j, i + j + k))
            k //= 2
        p *= 2
    return comps


_SORT16 = _oddeven_mergesort_network(PEER_TOPK)


def _sort16(x):
    x = list(x)
    for i, j in _SORT16:
        x[i], x[j] = jnp.maximum(x[i], x[j]), jnp.minimum(x[i], x[j])
    return x


def _merge_top16(x, y):
    x = [jnp.maximum(x[k], y[PEER_TOPK - 1 - k]) for k in range(PEER_TOPK)]
    d = PEER_TOPK // 2
    while d >= 1:
        for i in range(PEER_TOPK):
            if i & d == 0:
                x[i], x[i + d] = jnp.maximum(x[i], x[i + d]), jnp.minimum(x[i], x[i + d])
        d //= 2
    return x


def _sorted_top16(x):
    x = _sort16(x)
    for shift in (4, 2, 1):
        x = _merge_top16(x, [pltpu.roll(v, shift, axis=0) for v in x])
    return x


def _sel_kernel(h2_ref, wq_ref, keys_ref, e1_ref, cnt_ref, e2_ref, r2_ref,
                s1_scr, s2_scr, a_scr, b_scr, *, tt):
    ntc = tt // LANES
    neg = -jnp.inf
    q = jnp.dot(h2_ref[...], wq_ref[...], preferred_element_type=F32).astype(BF16)
    for h in range(PEER_HEADS):
        for c, scr in ((0, s1_scr), (1, s2_scr)):
            col = (h * 2 + c) * PEER_HALF
            scr[h] = lax.dot_general(keys_ref[h, c], q[:, col:col + PEER_HALF], NT,
                                     preferred_element_type=F32)

    def token_chunk(tc, carry):
        lanes = pl.ds(pl.multiple_of(tc * LANES, LANES), LANES)
        for h in range(PEER_HEADS):
            for side, (s_scr, v_scr) in enumerate(((s1_scr, a_scr), (s2_scr, b_scr))):
                s = s_scr[h, :, lanes]
                top = _sorted_top16([s[v * 8:(v + 1) * 8] for v in range(PEER_NKEYS // 8)])
                for k in range(PEER_TOPK):
                    v_scr[k, h:h + 1, lanes] = top[k][0:1]
                if side == 1:
                    rank = jnp.zeros(s.shape, F32)
                    for k in range(PEER_TOPK):
                        rank = jnp.where(top[k][0:1] > s, float(k + 1), rank)
                    r2_ref[h, :, lanes] = rank.astype(BF16)
        av = [a_scr[k, :, lanes] for k in range(PEER_TOPK)]
        bv = [b_scr[k, :, lanes] for k in range(PEER_TOPK)]
        cands = [av[a] + bv[b] for a, b in _CAND]
        pad = [jnp.full(cands[0].shape, neg, F32)] * (-len(cands) % PEER_TOPK)
        groups = [_sort16((cands + pad)[g:g + PEER_TOPK]) for g in range(0, len(cands) + len(pad), PEER_TOPK)]
        while len(groups) > 1:
            groups = [_merge_top16(groups[g], groups[g + 1]) for g in range(0, len(groups), 2)]
        t = groups[0][PEER_TOPK - 1]
        z = None
        for c in cands:
            term = jnp.where(c >= t, jnp.exp(c - cands[0]), 0.0)
            z = term if z is None else z + term
        zi = 1.0 / z
        cntr = []
        for a in range(PEER_TOPK):
            c = jnp.zeros(t.shape, F32)
            for b in range(PEER_TOPK):
                c = jnp.where(av[a] + bv[b] >= t, float(b + 1), c)
            cntr.append(c)
        for h in range(PEER_HEADS):
            s1 = s1_scr[h, :, lanes]
            s2 = s2_scr[h, :, lanes]
            cnt = jnp.zeros(s1.shape, F32)
            for k in reversed(range(PEER_TOPK)):
                cnt = jnp.where(s1 >= av[k][h:h + 1], cntr[k][h:h + 1], cnt)
            cnt_ref[h, :, lanes] = cnt
            e1_ref[h, :, lanes] = jnp.exp(s1 - av[0][h:h + 1]) * zi[h:h + 1]
            e2_ref[h, :, lanes] = jnp.exp(s2 - bv[0][h:h + 1]).astype(BF16)
        return carry

    lax.fori_loop(0, ntc, token_chunk, 0)


def _peer_select(h2, w_q, keys, tt):
    t, d = h2.shape
    shape = (PEER_HEADS, PEER_NKEYS, t)
    out = [jax.ShapeDtypeStruct(shape, F32)] * 2 + [jax.ShapeDtypeStruct(shape, BF16)] * 2
    ospec = pl.BlockSpec((PEER_HEADS, PEER_NKEYS, tt), lambda i: (0, 0, i))
    return pl.pallas_call(
        functools.partial(_sel_kernel, tt=tt),
        grid=(t // tt,),
        in_specs=[pl.BlockSpec((tt, d), lambda i: (i, 0)),
                  pl.BlockSpec(w_q.shape, lambda i: (0, 0)),
                  pl.BlockSpec(keys.shape, lambda i: (0, 0, 0, 0))],
        out_specs=[ospec] * 4,
        out_shape=out,
        scratch_shapes=[pltpu.VMEM((PEER_HEADS, PEER_NKEYS, tt), F32),
                        pltpu.VMEM((PEER_HEADS, PEER_NKEYS, tt), F32),
                        pltpu.VMEM((PEER_TOPK, PEER_HEADS, tt), F32),
                        pltpu.VMEM((PEER_TOPK, PEER_HEADS, tt), F32)],
        compiler_params=_cparams(("parallel",)),
        name="sel",
    )(h2, w_q, keys)


PEER_ECHUNK = 512
PEER_JSUB = 16
PEER_IGROUP = 2


def _gelu_tanh_bf16(x):
    c = np.sqrt(2.0 / np.pi)
    c0 = jnp.full(x.shape, c, F32).astype(x.dtype)
    c1 = jnp.full(x.shape, 0.044715 * c, F32).astype(x.dtype)
    t = jnp.tanh(x * (c0 + c1 * (x * x)))
    return x * (0.5 + 0.5 * t)


def _peer_kernel(h2_ref, u_ref, vt_ref, e1_ref, cnt_ref, e2_ref, r2_ref, x1_ref, gt2_ref, gpost2_ref,
                 y_ref, acc_ref, c_scr, er_scr, xt_scr, *, eb):
    e = pl.program_id(1)
    tt = h2_ref.shape[0]
    ipc = PEER_ECHUNK // PEER_NKEYS
    r2_off = PEER_HEADS * PEER_NKEYS + PEER_JSUB

    @pl.when(e == 0)
    def _():
        acc_ref[...] = jnp.zeros_like(acc_ref)
        xt_scr[...] = h2_ref[...].T
        for h in range(PEER_HEADS):
            er_scr[h * PEER_NKEYS:(h + 1) * PEER_NKEYS, 0:tt] = e2_ref[h].astype(F32)
            er_scr[r2_off + h * PEER_NKEYS:r2_off + (h + 1) * PEER_NKEYS, 0:tt] = r2_ref[h].astype(F32)

    bshape = (PEER_JSUB, LANES)
    for c in range(eb // PEER_ECHUNK):
        c0 = c * PEER_ECHUNK
        a = jnp.dot(u_ref[c0:c0 + PEER_ECHUNK, :], xt_scr[...], preferred_element_type=F32)
        for lc in range(tt // LANES):
            lanes = slice(lc * LANES, (lc + 1) * LANES)
            for ig in range(ipc // PEER_IGROUP):
                ils = [ig * PEER_IGROUP + t for t in range(PEER_IGROUP)]
                e1b, cntb = [], []
                for h in range(PEER_HEADS):
                    rows = [c * ipc + il for il in ils]
                    e1b.append([jnp.broadcast_to(e1_ref[h, i:i + 1, lanes], bshape).astype(BF16) for i in rows])
                    cntb.append([jnp.broadcast_to(cnt_ref[h, i:i + 1, lanes], bshape).astype(BF16) for i in rows])
                for jc in range(PEER_NKEYS // PEER_JSUB):
                    g = [None] * PEER_IGROUP
                    for h in range(PEER_HEADS):
                        j0 = h * PEER_NKEYS + jc * PEER_JSUB
                        r2v = er_scr[r2_off + j0:r2_off + j0 + PEER_JSUB, lanes].astype(BF16)
                        e2v = er_scr[j0:j0 + PEER_JSUB, lanes].astype(BF16)
                        for t in range(PEER_IGROUP):
                            term = jnp.where(r2v < cntb[h][t], e2v, 0.0) * e1b[h][t]
                            g[t] = term if g[t] is None else g[t] + term
                    for t, il in enumerate(ils):
                        r0 = il * PEER_NKEYS + jc * PEER_JSUB
                        act = _gelu_tanh_bf16(a[r0:r0 + PEER_JSUB, lanes].astype(BF16))
                        c_scr[c0 + r0:c0 + r0 + PEER_JSUB, lanes] = act * g[t]
    acc_ref[...] += jnp.dot(vt_ref[...], c_scr[...], preferred_element_type=F32)

    @pl.when(e == pl.num_programs(1) - 1)
    def _():
        f = acc_ref[...].T
        gt2 = gt2_ref[0] if len(gt2_ref.shape) == 3 else gt2_ref[...]
        y_ref[...] = x1_ref[...] + gt2 * _rms(f, gpost2_ref[...])


def _peer_dense(h2, x1, gt2, ub, vt, sel, g_post2, tt, eb, seq_len):
    t, d = h2.shape
    ne = ub.shape[0]
    e1, cnt, e2, r2 = sel
    ib = eb // PEER_NKEYS
    per_e = pl.BlockSpec((PEER_HEADS, ib, tt), lambda i, e: (0, e, i))
    per_t = pl.BlockSpec((PEER_HEADS, PEER_NKEYS, tt), lambda i, e: (0, 0, i))
    if seq_len == 1:
        gspec = pl.BlockSpec((tt, d), lambda i, e: (i, 0))
    else:
        assert seq_len % tt == 0
        tiles_per_seq = seq_len // tt
        gspec = pl.BlockSpec((1, 1, d), lambda i, e: (i // tiles_per_seq, 0, 0))
        gt2 = gt2.reshape(gt2.shape[0], 1, d)
    return pl.pallas_call(
        functools.partial(_peer_kernel, eb=eb),
        grid=(t // tt, ne // eb),
        in_specs=[pl.BlockSpec((tt, d), lambda i, e: (i, 0)),
                  pl.BlockSpec((eb, d), lambda i, e: (e, 0)),
                  pl.BlockSpec((d, eb), lambda i, e: (0, e)),
                  per_e, per_e, per_t, per_t,
                  pl.BlockSpec((tt, d), lambda i, e: (i, 0)),
                  gspec,
                  pl.BlockSpec((1, d), lambda i, e: (0, 0))],
        out_specs=pl.BlockSpec((tt, d), lambda i, e: (i, 0)),
        out_shape=jax.ShapeDtypeStruct((t, d), F32),
        scratch_shapes=[pltpu.VMEM((d, tt), F32), pltpu.VMEM((eb, tt), BF16),
                        pltpu.VMEM((2 * PEER_HEADS * PEER_NKEYS + PEER_JSUB, tt + LANES), F32),
                        pltpu.VMEM((d, tt), BF16)],
        compiler_params=_cparams(("parallel", "arbitrary")),
        name="peer",
    )(h2, ub, vt, e1, cnt, e2, r2, x1, gt2, g_post2.reshape(1, d))


def _prep_kernel(u_ref, v_ref, ub_ref, vt_ref):
    ub_ref[...] = u_ref[...].astype(BF16)
    vt_ref[...] = v_ref[...].T.astype(BF16)


def _prep_tables(u_tab, v_tab, blk=512):
    ne, d = u_tab.shape
    return pl.pallas_call(
        _prep_kernel,
        grid=(ne // blk,),
        in_specs=[pl.BlockSpec((blk, d), lambda i: (i, 0)), pl.BlockSpec((blk, d), lambda i: (i, 0))],
        out_specs=[pl.BlockSpec((blk, d), lambda i: (i, 0)), pl.BlockSpec((d, blk), lambda i: (0, i))],
        out_shape=[jax.ShapeDtypeStruct((ne, d), BF16), jax.ShapeDtypeStruct((d, ne), BF16)],
        compiler_params=_cparams(("parallel",)),
        name="prep",
    )(u_tab, v_tab)


PAST_LEN = 16384
MIX_TILE = 512
SEL_TILE = 512
PEER_TILE = 512
PEER_EBLOCK = 2048


def kernel(x_prompt, x_sample, c_prompt, c_sample, state_ret, cache_conv, w_ada, b_ada, g_pre1, g_post1, g_pre2, g_post2, w_in, w_out, conv_w, conv_b, w_q, sub_keys, u_tab, v_tab):
    depth = w_in.shape[0]
    assert depth == 1, "single-layer stack"
    l = 0
    b, s, d = x_prompt.shape
    n = x_sample.shape[0]
    assert x_sample.shape[1] == 1
    mod = _modulation(jnp.concatenate([c_prompt, c_sample], axis=0), w_ada[l], b_ada[l])
    mod_p, mod_s = mod[:b], mod[b:]
    w_in_b, w_out_b = w_in[l].astype(BF16), w_out[l].astype(BF16)
    w_q_b, keys_b = w_q[l].astype(BF16), sub_keys[l].astype(BF16)
    ub, vt = _prep_tables(u_tab[l], v_tab[l])

    x1p, h2p, sfin_p, cache_p = _prompt_mixer(
        x_prompt, mod_p.reshape(b, N_MOD, d), w_in_b, w_out_b,
        g_pre1[l], g_post1[l], g_pre2[l], conv_w[l], conv_b[l], min(MIX_TILE, s))
    x1s, h2s, sfin_s, cache_s = _sample_mixer(
        x_sample.reshape(n, d), mod_s, state_ret[l], cache_conv[l],
        PAST_LEN + jnp.arange(1, dtype=jnp.int32), w_in_b, w_out_b,
        g_pre1[l], g_post1[l], g_pre2[l], conv_w[l], conv_b[l])

    h2p = h2p.reshape(b * s, d)
    sel_p = _peer_select(h2p, w_q_b, keys_b, min(SEL_TILE, s))
    yp = _peer_dense(h2p, x1p.reshape(b * s, d), mod_p[:, 5 * d:6 * d], ub, vt, sel_p, g_post2[l],
                     min(PEER_TILE, s), PEER_EBLOCK, s)
    sel_s = _peer_select(h2s, w_q_b, keys_b, n)
    ys = _peer_dense(h2s, x1s, mod_s[:, 5 * d:6 * d], ub, vt, sel_s, g_post2[l], n, PEER_EBLOCK, 1)

    return (yp.reshape(b, s, d), ys.reshape(n, 1, d), sfin_p[None], cache_p[None],
            sfin_s[None], cache_s[None])
```

```python
import functools

import jax
import jax.numpy as jnp
import numpy as np
from jax import lax
from jax.experimental import pallas as pl
from jax.experimental.pallas import tpu as pltpu

F32 = jnp.float32
BF16 = jnp.bfloat16

LANES = 128
EPS = 1e-6
ROPE_BASE = 10000.0
RET_HEADS = 4
RET_DH = 128
RET_CHUNK = 128
CONV_K = 3
PEER_HEADS = 8
PEER_NKEYS = 128
PEER_HALF = 128
PEER_TOPK = 16
N_MOD = 6
VMEM_LIMIT = 56 * 1024 * 1024

NT = (((1,), (1,)), ((), ()))
TN = (((0,), (0,)), ((), ()))


def _cparams(sem):
    return pltpu.CompilerParams(dimension_semantics=sem, vmem_limit_bytes=VMEM_LIMIT)


def _rms(x, g):
    return x * lax.rsqrt(jnp.mean(x * x, axis=-1, keepdims=True) + EPS) * g


def _silu(x):
    return x / (1.0 + jnp.exp(-x))


def _rotary(x, cos, sin_signed):
    return x * cos + pltpu.roll(x, RET_DH // 2, axis=1) * sin_signed


def _mod_kernel(c_ref, w_ref, b_ref, o_ref):
    a = _silu(c_ref[...]).astype(BF16)
    o_ref[...] = jnp.dot(a, w_ref[...].astype(BF16), preferred_element_type=F32) + b_ref[...]


def _modulation(c, w_ada, b_ada):
    n, d = c.shape
    cols = w_ada.shape[1]
    blk = d
    return pl.pallas_call(
        _mod_kernel,
        grid=(cols // blk,),
        in_specs=[pl.BlockSpec((n, d), lambda j: (0, 0)),
                  pl.BlockSpec((d, blk), lambda j: (0, j)),
                  pl.BlockSpec((1, blk), lambda j: (0, j))],
        out_specs=pl.BlockSpec((n, blk), lambda j: (0, j)),
        out_shape=jax.ShapeDtypeStruct((n, cols), F32),
        compiler_params=_cparams(("parallel",)),
        name="mod",
    )(c, w_ada, b_ada.reshape(1, cols))


def _mix_kernel(x_ref, mod_ref, cos_ref, sin_ref, dec_ref, cdm_ref, kdm_ref, g128_ref,
                win_ref, wout_ref, gpre1_ref, gpost1_ref, gpre2_ref, cw_ref, cb_ref,
                x1_ref, h2_ref, sfin_ref, cache_ref,
                s_scr, ext_scr, mix_scr, *, tt):
    j = pl.program_id(1)
    rw = RET_HEADS * RET_DH
    cwid = cw_ref.shape[1]

    @pl.when(j == 0)
    def _():
        s_scr[...] = jnp.zeros_like(s_scr)
        ext_scr[0:8, :] = jnp.zeros((8, cwid), F32)

    x = x_ref[0]
    mod = mod_ref[0]
    sh1, sc1, gt1, sh2, sc2, gt2 = [mod[i:i + 1] for i in range(N_MOD)]
    h = (_rms(x, gpre1_ref[...]) * (1.0 + sc1) + sh1).astype(BF16)

    def proj(lo, width):
        return jnp.dot(h, win_ref[:, lo:lo + width], preferred_element_type=F32)

    pq = proj(0, rw)
    pk = proj(rw, rw)
    pv = proj(2 * rw, rw)
    pg = proj(3 * rw, rw)

    for c in range(tt // RET_CHUNK):
        r0 = c * RET_CHUNK
        cosb = cos_ref[r0:r0 + RET_CHUNK, :]
        sinb = sin_ref[r0:r0 + RET_CHUNK, :]
        for hh in range(RET_HEADS):
            l0 = hh * RET_DH
            q = _rotary(pq[r0:r0 + RET_CHUNK, l0:l0 + RET_DH], cosb, sinb)
            k = _rotary(pk[r0:r0 + RET_CHUNK, l0:l0 + RET_DH], cosb, sinb) * (RET_DH ** -0.5)
            v = pv[r0:r0 + RET_CHUNK, l0:l0 + RET_DH]
            g = pg[r0:r0 + RET_CHUNK, l0:l0 + RET_DH]
            qb, kb, vb = q.astype(BF16), k.astype(BF16), v.astype(BF16)
            s_old = s_scr[hh]
            scores = lax.dot_general(qb, kb, NT, preferred_element_type=F32) * dec_ref[hh]
            inner = jnp.dot(scores.astype(BF16), vb, preferred_element_type=F32)
            cross = jnp.dot(qb, s_old.astype(BF16), preferred_element_type=F32) * cdm_ref[hh]
            o = inner + cross
            kd = (k * kdm_ref[hh]).astype(BF16)
            s_scr[hh] = g128_ref[hh] * s_old + lax.dot_general(kd, vb, TN, preferred_element_type=F32)
            mu = jnp.mean(o, axis=-1, keepdims=True)
            oc = o - mu
            on = oc * lax.rsqrt(jnp.mean(oc * oc, axis=-1, keepdims=True) + EPS)
            mix_scr[r0:r0 + RET_CHUNK, l0:l0 + RET_DH] = (_silu(g) * on).astype(BF16)

    pb = proj(4 * rw, cwid)
    u = proj(4 * rw + cwid, cwid) * proj(4 * rw + 2 * cwid, cwid)
    ext_scr[8:8 + tt, :] = u
    y = (cb_ref[...] + cw_ref[0:1, :] * ext_scr[6:6 + tt, :]
         + cw_ref[1:2, :] * ext_scr[7:7 + tt, :] + cw_ref[2:3, :] * u)
    mix_scr[:, rw:rw + cwid] = (pb * y).astype(BF16)
    cache_ref[0] = ext_scr[tt + 6:tt + 8, :]
    ext_scr[0:8, :] = ext_scr[tt:tt + 8, :]

    m = jnp.dot(mix_scr[...], wout_ref[...], preferred_element_type=F32)
    x1 = x + gt1 * _rms(m, gpost1_ref[...])
    x1_ref[0] = x1
    h2_ref[0] = (_rms(x1, gpre2_ref[...]) * (1.0 + sc2) + sh2).astype(BF16)

    @pl.when(j == pl.num_programs(1) - 1)
    def _():
        sfin_ref[0] = s_scr[...]


def _decay_tables():
    lg = jnp.log(1.0 - 2.0 ** (-5.0 - jnp.arange(RET_HEADS, dtype=F32)))
    L = RET_CHUNK
    idx = jnp.arange(L, dtype=F32)
    rel = idx[:, None] - idx[None, :]
    dec = jnp.where(rel[None] >= 0, jnp.exp(lg[:, None, None] * jnp.maximum(rel, 0.0)[None]), 0.0)
    ones = jnp.ones((1, 1, RET_DH), F32)
    cdm = jnp.exp(lg[:, None] * (idx[None, :] + 1.0))[:, :, None] * ones
    kdm = jnp.exp(lg[:, None] * (L - 1.0 - idx[None, :]))[:, :, None] * ones
    g128 = jnp.exp(lg * L)[:, None, None] * jnp.ones((1, L, RET_DH), F32)
    return dec, cdm, kdm, g128


def _rope_tables(pos):
    half = RET_DH // 2
    freqs = ROPE_BASE ** (-jnp.arange(half, dtype=F32) / half)
    ang = pos.astype(F32)[:, None] * freqs[None, :]
    cos, sin = jnp.cos(ang), jnp.sin(ang)
    return jnp.concatenate([cos, cos], axis=-1), jnp.concatenate([-sin, sin], axis=-1)


def _prompt_mixer(x, mod, w_in, w_out, g_pre1, g_post1, g_pre2, conv_w, conv_b, tt):
    b, s, d = x.shape
    rw = RET_HEADS * RET_DH
    cwid = conv_w.shape[1]
    cos, sin = _rope_tables(jnp.arange(s, dtype=jnp.int32))
    dec, cdm, kdm, g128 = _decay_tables()
    tab = pl.BlockSpec((RET_HEADS, RET_CHUNK, RET_DH), lambda i, j: (0, 0, 0))
    row = pl.BlockSpec((1, d), lambda i, j: (0, 0))
    return pl.pallas_call(
        functools.partial(_mix_kernel, tt=tt),
        grid=(b, s // tt),
        in_specs=[pl.BlockSpec((1, tt, d), lambda i, j: (i, j, 0)),
                  pl.BlockSpec((1, N_MOD, d), lambda i, j: (i, 0, 0)),
                  pl.BlockSpec((tt, RET_DH), lambda i, j: (j, 0)),
                  pl.BlockSpec((tt, RET_DH), lambda i, j: (j, 0)),
                  tab, tab, tab, tab,
                  pl.BlockSpec(w_in.shape, lambda i, j: (0, 0)),
                  pl.BlockSpec(w_out.shape, lambda i, j: (0, 0)),
                  row, row, row,
                  pl.BlockSpec((CONV_K, cwid), lambda i, j: (0, 0)),
                  pl.BlockSpec((1, cwid), lambda i, j: (0, 0))],
        out_specs=[pl.BlockSpec((1, tt, d), lambda i, j: (i, j, 0)),
                   pl.BlockSpec((1, tt, d), lambda i, j: (i, j, 0)),
                   pl.BlockSpec((1, RET_HEADS, RET_DH, RET_DH), lambda i, j: (i, 0, 0, 0)),
                   pl.BlockSpec((1, CONV_K - 1, cwid), lambda i, j: (i, 0, 0))],
        out_shape=[jax.ShapeDtypeStruct((b, s, d), F32),
                   jax.ShapeDtypeStruct((b, s, d), BF16),
                   jax.ShapeDtypeStruct((b, RET_HEADS, RET_DH, RET_DH), F32),
                   jax.ShapeDtypeStruct((b, CONV_K - 1, cwid), F32)],
        scratch_shapes=[pltpu.VMEM((RET_HEADS, RET_DH, RET_DH), F32),
                        pltpu.VMEM((tt + 8, cwid), F32),
                        pltpu.VMEM((tt, rw + cwid), BF16)],
        compiler_params=_cparams(("parallel", "arbitrary")),
        name="mix",
    )(x, mod, cos, sin, dec, cdm, kdm, g128, w_in, w_out,
      g_pre1.reshape(1, d), g_post1.reshape(1, d), g_pre2.reshape(1, d),
      conv_w, conv_b.reshape(1, cwid))


SMP_BLOCK = 8


def _ret_gamma(hh):
    return 1.0 - 2.0 ** (-5.0 - hh)


def _smp_proj_kernel(x_ref, mod_ref, cos_ref, sin_ref, win_ref, gpre1_ref, cw_ref, cb_ref, cache_ref,
                     q_ref, k_ref, v_ref, g_ref, cout_ref, cache_out_ref):
    d = x_ref.shape[1]
    rw = RET_HEADS * RET_DH
    cwid = cw_ref.shape[1]
    x = x_ref[...]
    sh1 = mod_ref[:, 0:d]
    sc1 = mod_ref[:, d:2 * d]
    h = (_rms(x, gpre1_ref[...]) * (1.0 + sc1) + sh1).astype(BF16)
    p = jnp.dot(h, win_ref[...], preferred_element_type=F32)
    cosb, sinb = cos_ref[...], sin_ref[...]
    for hh in range(RET_HEADS):
        l0 = hh * RET_DH
        q_ref[:, l0:l0 + RET_DH] = _rotary(p[:, l0:l0 + RET_DH], cosb, sinb)
        k_ref[:, l0:l0 + RET_DH] = _rotary(p[:, rw + l0:rw + l0 + RET_DH], cosb, sinb) * (RET_DH ** -0.5)
    v_ref[...] = p[:, 2 * rw:3 * rw]
    g_ref[...] = p[:, 3 * rw:4 * rw]
    pb = p[:, 4 * rw:4 * rw + cwid]
    u = p[:, 4 * rw + cwid:4 * rw + 2 * cwid] * p[:, 4 * rw + 2 * cwid:4 * rw + 3 * cwid]
    buf0 = cache_ref[:, 0:cwid]
    buf1 = cache_ref[:, cwid:2 * cwid]
    y = cb_ref[...] + cw_ref[0:1, :] * buf0 + cw_ref[1:2, :] * buf1 + cw_ref[2:3, :] * u
    cout_ref[...] = pb * y
    cache_out_ref[:, 0:cwid] = buf1
    cache_out_ref[:, cwid:2 * cwid] = u


def _smp_state_kernel(q_ref, k_ref, v_ref, s0_ref, o_ref, snew_ref):
    nb = SMP_BLOCK
    row = lax.broadcasted_iota(jnp.int32, (nb, RET_DH), 0)
    zpad = jnp.zeros((RET_CHUNK - nb, RET_DH), F32)
    for hh in range(RET_HEADS):
        l0 = hh * RET_DH
        gam = _ret_gamma(hh)
        qb = q_ref[:, l0:l0 + RET_DH].astype(BF16)
        kb = k_ref[:, l0:l0 + RET_DH].astype(BF16)
        vb = v_ref[:, l0:l0 + RET_DH].astype(BF16)
        s_list = [s0_ref[n, hh] for n in range(nb)]
        scat = jnp.concatenate(s_list, axis=1).astype(BF16)
        res = jnp.dot(qb, scat, preferred_element_type=F32)
        cross = jnp.zeros((nb, RET_DH), F32)
        for n in range(nb):
            cross = cross + jnp.where(row == n, res[:, n * RET_DH:(n + 1) * RET_DH], 0.0)
        sc = jnp.sum(qb.astype(F32) * kb.astype(F32), axis=-1, keepdims=True)
        o_ref[:, l0:l0 + RET_DH] = sc.astype(BF16).astype(F32) * vb.astype(F32) + cross * gam
        kpad = jnp.concatenate([kb.astype(F32), zpad], axis=0).astype(BF16)
        vf = vb.astype(F32)
        w = jnp.concatenate([jnp.where(row == n, vf, 0.0) for n in range(nb)], axis=1)
        wpad = jnp.concatenate([w, jnp.zeros((RET_CHUNK - nb, nb * RET_DH), F32)], axis=0).astype(BF16)
        kv = lax.dot_general(kpad, wpad, TN, preferred_element_type=F32)
        for n in range(nb):
            snew_ref[n, hh] = gam * s_list[n] + kv[:, n * RET_DH:(n + 1) * RET_DH]


def _smp_post_kernel(x_ref, mod_ref, o_ref, g_ref, cout_ref, wout_ref, gpost1_ref, gpre2_ref,
                     x1_ref, h2_ref):
    d = x_ref.shape[1]
    rw = RET_HEADS * RET_DH
    x = x_ref[...]
    gt1 = mod_ref[:, 2 * d:3 * d]
    sh2 = mod_ref[:, 3 * d:4 * d]
    sc2 = mod_ref[:, 4 * d:5 * d]
    m = jnp.dot(cout_ref[...].astype(BF16), wout_ref[rw:, :], preferred_element_type=F32)
    for hh in range(RET_HEADS):
        l0 = hh * RET_DH
        o = o_ref[:, l0:l0 + RET_DH]
        oc = o - jnp.mean(o, axis=-1, keepdims=True)
        on = oc * lax.rsqrt(jnp.mean(oc * oc, axis=-1, keepdims=True) + EPS)
        r = (_silu(g_ref[:, l0:l0 + RET_DH]) * on).astype(BF16)
        m = m + jnp.dot(r, wout_ref[l0:l0 + RET_DH, :], preferred_element_type=F32)
    x1 = x + gt1 * _rms(m, gpost1_ref[...])
    x1_ref[...] = x1
    h2_ref[...] = (_rms(x1, gpre2_ref[...]) * (1.0 + sc2) + sh2).astype(BF16)


def _sample_mixer(x, mod, state, cache, pos, w_in, w_out, g_pre1, g_post1, g_pre2, conv_w, conv_b):
    n, d = x.shape
    rw = RET_HEADS * RET_DH
    cwid = conv_w.shape[1]
    cos, sin = _rope_tables(pos)
    f = jax.ShapeDtypeStruct
    q, k, v, g, cout, cache_out = pl.pallas_call(
        _smp_proj_kernel,
        out_shape=[f((n, rw), F32)] * 4 + [f((n, cwid), F32), f((n, (CONV_K - 1) * cwid), F32)],
        compiler_params=pltpu.CompilerParams(vmem_limit_bytes=VMEM_LIMIT),
        name="smp_proj",
    )(x, mod, cos, sin, w_in, g_pre1.reshape(1, d), conv_w, conv_b.reshape(1, cwid),
      cache.reshape(n, (CONV_K - 1) * cwid))
    nb = SMP_BLOCK
    vec = pl.BlockSpec((nb, rw), lambda i: (i, 0))
    st = pl.BlockSpec((nb, RET_HEADS, RET_DH, RET_DH), lambda i: (i, 0, 0, 0))
    o, snew = pl.pallas_call(
        _smp_state_kernel,
        grid=(n // nb,),
        in_specs=[vec, vec, vec, st],
        out_specs=[vec, st],
        out_shape=[f((n, rw), F32), f(state.shape, F32)],
        compiler_params=_cparams(("parallel",)),
        name="smp_state",
    )(q, k, v, state)
    x1, h2 = pl.pallas_call(
        _smp_post_kernel,
        out_shape=[f((n, d), F32), f((n, d), BF16)],
        compiler_params=pltpu.CompilerParams(vmem_limit_bytes=VMEM_LIMIT),
        name="smp_post",
    )(x, mod, o, g, cout, w_out, g_post1.reshape(1, d), g_pre2.reshape(1, d))
    return x1, h2, snew, cache_out.reshape(n, CONV_K - 1, cwid)


_CAND = [(a, b) for a in range(PEER_TOPK) for b in range(PEER_TOPK) if (a + 1) * (b + 1) <= PEER_TOPK]


def _oddeven_mergesort_network(n):
    comps = []
    p = 1
    while p < n:
        k = p
        while k >= 1:
            for j in range(k % p, n - k, 2 * k):
                for i in range(min(k, n - j - k)):
                    if (i + j) // (2 * p) == (i + j + k) // (2 * p):
                        comps.append((i + j, i + j + k))
            k //= 2
        p *= 2
    return comps


_SORT16 = _oddeven_mergesort_network(PEER_TOPK)


def _sort16(x):
    x = list(x)
    for i, j in _SORT16:
        x[i], x[j] = jnp.maximum(x[i], x[j]), jnp.minimum(x[i], x[j])
    return x


def _merge_top16(x, y):
    x = [jnp.maximum(x[k], y[PEER_TOPK - 1 - k]) for k in range(PEER_TOPK)]
    d = PEER_TOPK // 2
    while d >= 1:
        for i in range(PEER_TOPK):
            if i & d == 0:
                x[i], x[i + d] = jnp.maximum(x[i], x[i + d]), jnp.minimum(x[i], x[i + d])
        d //= 2
    return x


def _sorted_top16(x):
    x = _sort16(x)
    for shift in (4, 2, 1):
        x = _merge_top16(x, [pltpu.roll(v, shift, axis=0) for v in x])
    return x


def _sel_kernel(h2_ref, wq_ref, keys_ref, e1_ref, cnt_ref, e2_ref, r2_ref,
                s1_scr, s2_scr, a_scr, b_scr, *, tt):
    ntc = tt // LANES
    neg = -jnp.inf
    q = jnp.dot(h2_ref[...], wq_ref[...], preferred_element_type=F32).astype(BF16)
    for h in range(PEER_HEADS):
        for c, scr in ((0, s1_scr), (1, s2_scr)):
            col = (h * 2 + c) * PEER_HALF
            scr[h] = lax.dot_general(keys_ref[h, c], q[:, col:col + PEER_HALF], NT,
                                     preferred_element_type=F32)

    def token_chunk(tc, carry):
        lanes = pl.ds(pl.multiple_of(tc * LANES, LANES), LANES)
        for h in range(PEER_HEADS):
            for side, (s_scr, v_scr) in enumerate(((s1_scr, a_scr), (s2_scr, b_scr))):
                s = s_scr[h, :, lanes]
                top = _sorted_top16([s[v * 8:(v + 1) * 8] for v in range(PEER_NKEYS // 8)])
                for k in range(PEER_TOPK):
                    v_scr[k, h:h + 1, lanes] = top[k][0:1]
                if side == 1:
                    rank = jnp.zeros(s.shape, F32)
                    for k in range(PEER_TOPK):
                        rank = jnp.where(top[k][0:1] > s, float(k + 1), rank)
                    r2_ref[h, :, lanes] = rank.astype(BF16)
        av = [a_scr[k, :, lanes] for k in range(PEER_TOPK)]
        bv = [b_scr[k, :, lanes] for k in range(PEER_TOPK)]
        cands = [av[a] + bv[b] for a, b in _CAND]
        pad = [jnp.full(cands[0].shape, neg, F32)] * (-len(cands) % PEER_TOPK)
        groups = [_sort16((cands + pad)[g:g + PEER_TOPK]) for g in range(0, len(cands) + len(pad), PEER_TOPK)]
        while len(groups) > 1:
            groups = [_merge_top16(groups[g], groups[g + 1]) for g in range(0, len(groups), 2)]
        t = groups[0][PEER_TOPK - 1]
        z = None
        for c in cands:
            term = jnp.where(c >= t, jnp.exp(c - cands[0]), 0.0)
            z = term if z is None else z + term
        zi = 1.0 / z
        cntr = []
        for a in range(PEER_TOPK):
            c = jnp.zeros(t.shape, F32)
            for b in range(PEER_TOPK):
                c = jnp.where(av[a] + bv[b] >= t, float(b + 1), c)
            cntr.append(c)
        for h in range(PEER_HEADS):
            s1 = s1_scr[h, :, lanes]
            s2 = s2_scr[h, :, lanes]
            cnt = jnp.zeros(s1.shape, F32)
            for k in reversed(range(PEER_TOPK)):
                cnt = jnp.where(s1 >= av[k][h:h + 1], cntr[k][h:h + 1], cnt)
            cnt_ref[h, :, lanes] = cnt
            e1_ref[h, :, lanes] = jnp.exp(s1 - av[0][h:h + 1]) * zi[h:h + 1]
            e2_ref[h, :, lanes] = jnp.exp(s2 - bv[0][h:h + 1]).astype(BF16)
        return carry

    lax.fori_loop(0, ntc, token_chunk, 0)


def _peer_select(h2, w_q, keys, tt):
    t, d = h2.shape
    shape = (PEER_HEADS, PEER_NKEYS, t)
    out = [jax.ShapeDtypeStruct(shape, F32)] * 2 + [jax.ShapeDtypeStruct(shape, BF16)] * 2
    ospec = pl.BlockSpec((PEER_HEADS, PEER_NKEYS, tt), lambda i: (0, 0, i))
    return pl.pallas_call(
        functools.partial(_sel_kernel, tt=tt),
        grid=(t // tt,),
        in_specs=[pl.BlockSpec((tt, d), lambda i: (i, 0)),
                  pl.BlockSpec(w_q.shape, lambda i: (0, 0)),
                  pl.BlockSpec(keys.shape, lambda i: (0, 0, 0, 0))],
        out_specs=[ospec] * 4,
        out_shape=out,
        scratch_shapes=[pltpu.VMEM((PEER_HEADS, PEER_NKEYS, tt), F32),
                        pltpu.VMEM((PEER_HEADS, PEER_NKEYS, tt), F32),
                        pltpu.VMEM((PEER_TOPK, PEER_HEADS, tt), F32),
                        pltpu.VMEM((PEER_TOPK, PEER_HEADS, tt), F32)],
        compiler_params=_cparams(("parallel",)),
        name="sel",
    )(h2, w_q, keys)


PEER_ECHUNK = 512
PEER_JSUB = 16
PEER_IGROUP = 4


def _gelu_tanh_bf16(x):
    c = np.sqrt(2.0 / np.pi)
    c0 = jnp.full(x.shape, c, F32).astype(x.dtype)
    c1 = jnp.full(x.shape, 0.044715 * c, F32).astype(x.dtype)
    t = jnp.tanh(x * (c0 + c1 * (x * x)))
    return x * (0.5 + 0.5 * t)


def _peer_kernel(h2_ref, u_ref, vt_ref, e1_ref, cnt_ref, e2_ref, r2_ref, x1_ref, gt2_ref, gpost2_ref,
                 y_ref, acc_ref, c_scr, er_scr, xt_scr, *, eb):
    e = pl.program_id(1)
    tt = h2_ref.shape[0]
    ipc = PEER_ECHUNK // PEER_NKEYS
    r2_off = PEER_HEADS * PEER_NKEYS + PEER_JSUB

    @pl.when(e == 0)
    def _():
        acc_ref[...] = jnp.zeros_like(acc_ref)
        xt_scr[...] = h2_ref[...].T
        for h in range(PEER_HEADS):
            er_scr[h * PEER_NKEYS:(h + 1) * PEER_NKEYS, 0:tt] = e2_ref[h].astype(F32)
            er_scr[r2_off + h * PEER_NKEYS:r2_off + (h + 1) * PEER_NKEYS, 0:tt] = r2_ref[h].astype(F32)

    bshape = (PEER_JSUB, LANES)
    for c in range(eb // PEER_ECHUNK):
        c0 = c * PEER_ECHUNK
        a = jnp.dot(u_ref[c0:c0 + PEER_ECHUNK, :], xt_scr[...], preferred_element_type=F32)
        for lc in range(tt // LANES):
            lanes = slice(lc * LANES, (lc + 1) * LANES)
            for ig in range(ipc // PEER_IGROUP):
                ils = [ig * PEER_IGROUP + t for t in range(PEER_IGROUP)]
                e1b, cntb = [], []
                for h in range(PEER_HEADS):
                    rows = [c * ipc + il for il in ils]
                    e1b.append([jnp.broadcast_to(e1_ref[h, i:i + 1, lanes], bshape).astype(BF16) for i in rows])
                    cntb.append([jnp.broadcast_to(cnt_ref[h, i:i + 1, lanes], bshape).astype(BF16) for i in rows])
                for jc in range(PEER_NKEYS // PEER_JSUB):
                    g = [None] * PEER_IGROUP
                    for h in range(PEER_HEADS):
                        j0 = h * PEER_NKEYS + jc * PEER_JSUB
                        r2v = er_scr[r2_off + j0:r2_off + j0 + PEER_JSUB, lanes].astype(BF16)
                        e2v = er_scr[j0:j0 + PEER_JSUB, lanes].astype(BF16)
                        for t in range(PEER_IGROUP):
                            term = jnp.where(r2v < cntb[h][t], e2v, 0.0) * e1b[h][t]
                            g[t] = term if g[t] is None else g[t] + term
                    for t, il in enumerate(ils):
                        r0 = il * PEER_NKEYS + jc * PEER_JSUB
                        act = _gelu_tanh_bf16(a[r0:r0 + PEER_JSUB, lanes].astype(BF16))
                        c_scr[c0 + r0:c0 + r0 + PEER_JSUB, lanes] = act * g[t]
    acc_ref[...] += jnp.dot(vt_ref[...], c_scr[...], preferred_element_type=F32)

    @pl.when(e == pl.num_programs(1) - 1)
    def _():
        f = acc_ref[...].T
        gt2 = gt2_ref[0] if len(gt2_ref.shape) == 3 else gt2_ref[...]
        y_ref[...] = x1_ref[...] + gt2 * _rms(f, gpost2_ref[...])


def _peer_dense(h2, x1, gt2, ub, vt, sel, g_post2, tt, eb, seq_len):
    t, d = h2.shape
    ne = ub.shape[0]
    e1, cnt, e2, r2 = sel
    ib = eb // PEER_NKEYS
    per_e = pl.BlockSpec((PEER_HEADS, ib, tt), lambda i, e: (0, e, i))
    per_t = pl.BlockSpec((PEER_HEADS, PEER_NKEYS, tt), lambda i, e: (0, 0, i))
    if seq_len == 1:
        gspec = pl.BlockSpec((tt, d), lambda i, e: (i, 0))
    else:
        assert seq_len % tt == 0
        tiles_per_seq = seq_len // tt
        gspec = pl.BlockSpec((1, 1, d), lambda i, e: (i // tiles_per_seq, 0, 0))
        gt2 = gt2.reshape(gt2.shape[0], 1, d)
    return pl.pallas_call(
        functools.partial(_peer_kernel, eb=eb),
        grid=(t // tt, ne // eb),
        in_specs=[pl.BlockSpec((tt, d), lambda i, e: (i, 0)),
                  pl.BlockSpec((eb, d), lambda i, e: (e, 0)),
                  pl.BlockSpec((d, eb), lambda i, e: (0, e)),
                  per_e, per_e, per_t, per_t,
                  pl.BlockSpec((tt, d), lambda i, e: (i, 0)),
                  gspec,
                  pl.BlockSpec((1, d), lambda i, e: (0, 0))],
        out_specs=pl.BlockSpec((tt, d), lambda i, e: (i, 0)),
        out_shape=jax.ShapeDtypeStruct((t, d), F32),
        scratch_shapes=[pltpu.VMEM((d, tt), F32), pltpu.VMEM((eb, tt), BF16),
                        pltpu.VMEM((2 * PEER_HEADS * PEER_NKEYS + PEER_JSUB, tt + LANES), F32),
                        pltpu.VMEM((d, tt), BF16)],
        compiler_params=_cparams(("parallel", "arbitrary")),
        name="peer",
    )(h2, ub, vt, e1, cnt, e2, r2, x1, gt2, g_post2.reshape(1, d))


def _prep_kernel(u_ref, v_ref, ub_ref, vt_ref):
    ub_ref[...] = u_ref[...].astype(BF16)
    vt_ref[...] = v_ref[...].T.astype(BF16)


def _prep_tables(u_tab, v_tab, blk=512):
    ne, d = u_tab.shape
    return pl.pallas_call(
        _prep_kernel,
        grid=(ne // blk,),
        in_specs=[pl.BlockSpec((blk, d), lambda i: (i, 0)), pl.BlockSpec((blk, d), lambda i: (i, 0))],
        out_specs=[pl.BlockSpec((blk, d), lambda i: (i, 0)), pl.BlockSpec((d, blk), lambda i: (0, i))],
        out_shape=[jax.ShapeDtypeStruct((ne, d), BF16), jax.ShapeDtypeStruct((d, ne), BF16)],
        compiler_params=_cparams(("parallel",)),
        name="prep",
    )(u_tab, v_tab)


PAST_LEN = 16384
MIX_TILE = 512
SEL_TILE = 512
PEER_TILE = 512
PEER_EBLOCK = 2048


def kernel(x_prompt, x_sample, c_prompt, c_sample, state_ret, cache_conv, w_ada, b_ada, g_pre1, g_post1, g_pre2, g_post2, w_in, w_out, conv_w, conv_b, w_q, sub_keys, u_tab, v_tab):
    depth = w_in.shape[0]
    assert depth == 1, "single-layer stack"
    l = 0
    b, s, d = x_prompt.shape
    n = x_sample.shape[0]
    assert x_sample.shape[1] == 1
    mod = _modulation(jnp.concatenate([c_prompt, c_sample], axis=0), w_ada[l], b_ada[l])
    mod_p, mod_s = mod[:b], mod[b:]
    w_in_b, w_out_b = w_in[l].astype(BF16), w_out[l].astype(BF16)
    w_q_b, keys_b = w_q[l].astype(BF16), sub_keys[l].astype(BF16)
    ub, vt = _prep_tables(u_tab[l], v_tab[l])

    x1p, h2p, sfin_p, cache_p = _prompt_mixer(
        x_prompt, mod_p.reshape(b, N_MOD, d), w_in_b, w_out_b,
        g_pre1[l], g_post1[l], g_pre2[l], conv_w[l], conv_b[l], min(MIX_TILE, s))
    x1s, h2s, sfin_s, cache_s = _sample_mixer(
        x_sample.reshape(n, d), mod_s, state_ret[l], cache_conv[l],
        PAST_LEN + jnp.arange(1, dtype=jnp.int32), w_in_b, w_out_b,
        g_pre1[l], g_post1[l], g_pre2[l], conv_w[l], conv_b[l])

    h2p = h2p.reshape(b * s, d)
    sel_p = _peer_select(h2p, w_q_b, keys_b, min(SEL_TILE, s))
    yp = _peer_dense(h2p, x1p.reshape(b * s, d), mod_p[:, 5 * d:6 * d], ub, vt, sel_p, g_post2[l],
                     min(PEER_TILE, s), PEER_EBLOCK, s)
    sel_s = _peer_select(h2s, w_q_b, keys_b, n)
    ys = _peer_dense(h2s, x1s, mod_s[:, 5 * d:6 * d], ub, vt, sel_s, g_post2[l], n, PEER_EBLOCK, 1)

    return (yp.reshape(b, s, d), ys.reshape(n, 1, d), sfin_p[None], cache_p[None],
            sfin_s[None], cache_s[None])
```

```python
import functools

import jax
import jax.numpy as jnp
import numpy as np
from jax import lax
from jax.experimental import pallas as pl
from jax.experimental.pallas import tpu as pltpu

F32 = jnp.float32
BF16 = jnp.bfloat16

LANES = 128
EPS = 1e-6
ROPE_BASE = 10000.0
RET_HEADS = 4
RET_DH = 128
RET_CHUNK = 128
CONV_K = 3
PEER_HEADS = 8
PEER_NKEYS = 128
PEER_HALF = 128
PEER_TOPK = 16
N_MOD = 6
VMEM_LIMIT = 56 * 1024 * 1024

NT = (((1,), (1,)), ((), ()))
TN = (((0,), (0,)), ((), ()))


def _cparams(sem):
    return pltpu.CompilerParams(dimension_semantics=sem, vmem_limit_bytes=VMEM_LIMIT)


def _rms(x, g):
    return x * lax.rsqrt(jnp.mean(x * x, axis=-1, keepdims=True) + EPS) * g


def _silu(x):
    return x / (1.0 + jnp.exp(-x))


def _rotary(x, cos, sin_signed):
    return x * cos + pltpu.roll(x, RET_DH // 2, axis=1) * sin_signed


def _mod_kernel(c_ref, w_ref, b_ref, o_ref):
    a = _silu(c_ref[...]).astype(BF16)
    o_ref[...] = jnp.dot(a, w_ref[...].astype(BF16), preferred_element_type=F32) + b_ref[...]


def _modulation(c, w_ada, b_ada):
    n, d = c.shape
    cols = w_ada.shape[1]
    blk = d
    return pl.pallas_call(
        _mod_kernel,
        grid=(cols // blk,),
        in_specs=[pl.BlockSpec((n, d), lambda j: (0, 0)),
                  pl.BlockSpec((d, blk), lambda j: (0, j)),
                  pl.BlockSpec((1, blk), lambda j: (0, j))],
        out_specs=pl.BlockSpec((n, blk), lambda j: (0, j)),
        out_shape=jax.ShapeDtypeStruct((n, cols), F32),
        compiler_params=_cparams(("parallel",)),
        name="mod",
    )(c, w_ada, b_ada.reshape(1, cols))


def _mix_kernel(x_ref, mod_ref, cos_ref, sin_ref, dec_ref, cdm_ref, kdm_ref, g128_ref,
                win_ref, wout_ref, gpre1_ref, gpost1_ref, gpre2_ref, cw_ref, cb_ref,
                x1_ref, h2_ref, sfin_ref, cache_ref,
                s_scr, ext_scr, mix_scr, *, tt):
    j = pl.program_id(1)
    rw = RET_HEADS * RET_DH
    cwid = cw_ref.shape[1]

    @pl.when(j == 0)
    def _():
        s_scr[...] = jnp.zeros_like(s_scr)
        ext_scr[0:8, :] = jnp.zeros((8, cwid), F32)

    x = x_ref[0]
    mod = mod_ref[0]
    sh1, sc1, gt1, sh2, sc2, gt2 = [mod[i:i + 1] for i in range(N_MOD)]
    h = (_rms(x, gpre1_ref[...]) * (1.0 + sc1) + sh1).astype(BF16)

    def proj(lo, width):
        return jnp.dot(h, win_ref[:, lo:lo + width], preferred_element_type=F32)

    pq = proj(0, rw)
    pk = proj(rw, rw)
    pv = proj(2 * rw, rw)
    pg = proj(3 * rw, rw)

    for c in range(tt // RET_CHUNK):
        r0 = c * RET_CHUNK
        cosb = cos_ref[r0:r0 + RET_CHUNK, :]
        sinb = sin_ref[r0:r0 + RET_CHUNK, :]
        for hh in range(RET_HEADS):
            l0 = hh * RET_DH
            q = _rotary(pq[r0:r0 + RET_CHUNK, l0:l0 + RET_DH], cosb, sinb)
            k = _rotary(pk[r0:r0 + RET_CHUNK, l0:l0 + RET_DH], cosb, sinb) * (RET_DH ** -0.5)
            v = pv[r0:r0 + RET_CHUNK, l0:l0 + RET_DH]
            g = pg[r0:r0 + RET_CHUNK, l0:l0 + RET_DH]
            qb, kb, vb = q.astype(BF16), k.astype(BF16), v.astype(BF16)
            s_old = s_scr[hh]
            scores = lax.dot_general(qb, kb, NT, preferred_element_type=F32) * dec_ref[hh]
            inner = jnp.dot(scores.astype(BF16), vb, preferred_element_type=F32)
            cross = jnp.dot(qb, s_old.astype(BF16), preferred_element_type=F32) * cdm_ref[hh]
            o = inner + cross
            kd = (k * kdm_ref[hh]).astype(BF16)
            s_scr[hh] = g128_ref[hh] * s_old + lax.dot_general(kd, vb, TN, preferred_element_type=F32)
            mu = jnp.mean(o, axis=-1, keepdims=True)
            oc = o - mu
            on = oc * lax.rsqrt(jnp.mean(oc * oc, axis=-1, keepdims=True) + EPS)
            mix_scr[r0:r0 + RET_CHUNK, l0:l0 + RET_DH] = (_silu(g) * on).astype(BF16)

    pb = proj(4 * rw, cwid)
    u = proj(4 * rw + cwid, cwid) * proj(4 * rw + 2 * cwid, cwid)
    ext_scr[8:8 + tt, :] = u
    y = (cb_ref[...] + cw_ref[0:1, :] * ext_scr[6:6 + tt, :]
         + cw_ref[1:2, :] * ext_scr[7:7 + tt, :] + cw_ref[2:3, :] * u)
    mix_scr[:, rw:rw + cwid] = (pb * y).astype(BF16)
    cache_ref[0] = ext_scr[tt + 6:tt + 8, :]
    ext_scr[0:8, :] = ext_scr[tt:tt + 8, :]

    m = jnp.dot(mix_scr[...], wout_ref[...], preferred_element_type=F32)
    x1 = x + gt1 * _rms(m, gpost1_ref[...])
    x1_ref[0] = x1
    h2_ref[0] = (_rms(x1, gpre2_ref[...]) * (1.0 + sc2) + sh2).astype(BF16)

    @pl.when(j == pl.num_programs(1) - 1)
    def _():
        sfin_ref[0] = s_scr[...]


def _decay_tables():
    lg = jnp.log(1.0 - 2.0 ** (-5.0 - jnp.arange(RET_HEADS, dtype=F32)))
    L = RET_CHUNK
    idx = jnp.arange(L, dtype=F32)
    rel = idx[:, None] - idx[None, :]
    dec = jnp.where(rel[None] >= 0, jnp.exp(lg[:, None, None] * jnp.maximum(rel, 0.0)[None]), 0.0)
    ones = jnp.ones((1, 1, RET_DH), F32)
    cdm = jnp.exp(lg[:, None] * (idx[None, :] + 1.0))[:, :, None] * ones
    kdm = jnp.exp(lg[:, None] * (L - 1.0 - idx[None, :]))[:, :, None] * ones
    g128 = jnp.exp(lg * L)[:, None, None] * jnp.ones((1, L, RET_DH), F32)
    return dec, cdm, kdm, g128


def _rope_tables(pos):
    half = RET_DH // 2
    freqs = ROPE_BASE ** (-jnp.arange(half, dtype=F32) / half)
    ang = pos.astype(F32)[:, None] * freqs[None, :]
    cos, sin = jnp.cos(ang), jnp.sin(ang)
    return jnp.concatenate([cos, cos], axis=-1), jnp.concatenate([-sin, sin], axis=-1)


def _prompt_mixer(x, mod, w_in, w_out, g_pre1, g_post1, g_pre2, conv_w, conv_b, tt):
    b, s, d = x.shape
    rw = RET_HEADS * RET_DH
    cwid = conv_w.shape[1]
    cos, sin = _rope_tables(jnp.arange(s, dtype=jnp.int32))
    dec, cdm, kdm, g128 = _decay_tables()
    tab = pl.BlockSpec((RET_HEADS, RET_CHUNK, RET_DH), lambda i, j: (0, 0, 0))
    row = pl.BlockSpec((1, d), lambda i, j: (0, 0))
    return pl.pallas_call(
        functools.partial(_mix_kernel, tt=tt),
        grid=(b, s // tt),
        in_specs=[pl.BlockSpec((1, tt, d), lambda i, j: (i, j, 0)),
                  pl.BlockSpec((1, N_MOD, d), lambda i, j: (i, 0, 0)),
                  pl.BlockSpec((tt, RET_DH), lambda i, j: (j, 0)),
                  pl.BlockSpec((tt, RET_DH), lambda i, j: (j, 0)),
                  tab, tab, tab, tab,
                  pl.BlockSpec(w_in.shape, lambda i, j: (0, 0)),
                  pl.BlockSpec(w_out.shape, lambda i, j: (0, 0)),
                  row, row, row,
                  pl.BlockSpec((CONV_K, cwid), lambda i, j: (0, 0)),
                  pl.BlockSpec((1, cwid), lambda i, j: (0, 0))],
        out_specs=[pl.BlockSpec((1, tt, d), lambda i, j: (i, j, 0)),
                   pl.BlockSpec((1, tt, d), lambda i, j: (i, j, 0)),
                   pl.BlockSpec((1, RET_HEADS, RET_DH, RET_DH), lambda i, j: (i, 0, 0, 0)),
                   pl.BlockSpec((1, CONV_K - 1, cwid), lambda i, j: (i, 0, 0))],
        out_shape=[jax.ShapeDtypeStruct((b, s, d), F32),
                   jax.ShapeDtypeStruct((b, s, d), BF16),
                   jax.ShapeDtypeStruct((b, RET_HEADS, RET_DH, RET_DH), F32),
                   jax.ShapeDtypeStruct((b, CONV_K - 1, cwid), F32)],
        scratch_shapes=[pltpu.VMEM((RET_HEADS, RET_DH, RET_DH), F32),
                        pltpu.VMEM((tt + 8, cwid), F32),
                        pltpu.VMEM((tt, rw + cwid), BF16)],
        compiler_params=_cparams(("parallel", "arbitrary")),
        name="mix",
    )(x, mod, cos, sin, dec, cdm, kdm, g128, w_in, w_out,
      g_pre1.reshape(1, d), g_post1.reshape(1, d), g_pre2.reshape(1, d),
      conv_w, conv_b.reshape(1, cwid))


SMP_BLOCK = 8


def _ret_gamma(hh):
    return 1.0 - 2.0 ** (-5.0 - hh)


def _smp_proj_kernel(x_ref, mod_ref, cos_ref, sin_ref, win_ref, gpre1_ref, cw_ref, cb_ref, cache_ref,
                     q_ref, k_ref, v_ref, g_ref, cout_ref, cache_out_ref):
    d = x_ref.shape[1]
    rw = RET_HEADS * RET_DH
    cwid = cw_ref.shape[1]
    x = x_ref[...]
    sh1 = mod_ref[:, 0:d]
    sc1 = mod_ref[:, d:2 * d]
    h = (_rms(x, gpre1_ref[...]) * (1.0 + sc1) + sh1).astype(BF16)
    p = jnp.dot(h, win_ref[...], preferred_element_type=F32)
    cosb, sinb = cos_ref[...], sin_ref[...]
    for hh in range(RET_HEADS):
        l0 = hh * RET_DH
        q_ref[:, l0:l0 + RET_DH] = _rotary(p[:, l0:l0 + RET_DH], cosb, sinb)
        k_ref[:, l0:l0 + RET_DH] = _rotary(p[:, rw + l0:rw + l0 + RET_DH], cosb, sinb) * (RET_DH ** -0.5)
    v_ref[...] = p[:, 2 * rw:3 * rw]
    g_ref[...] = p[:, 3 * rw:4 * rw]
    pb = p[:, 4 * rw:4 * rw + cwid]
    u = p[:, 4 * rw + cwid:4 * rw + 2 * cwid] * p[:, 4 * rw + 2 * cwid:4 * rw + 3 * cwid]
    buf0 = cache_ref[:, 0:cwid]
    buf1 = cache_ref[:, cwid:2 * cwid]
    y = cb_ref[...] + cw_ref[0:1, :] * buf0 + cw_ref[1:2, :] * buf1 + cw_ref[2:3, :] * u
    cout_ref[...] = pb * y
    cache_out_ref[:, 0:cwid] = buf1
    cache_out_ref[:, cwid:2 * cwid] = u


def _smp_state_kernel(q_ref, k_ref, v_ref, s0_ref, o_ref, snew_ref):
    nb = SMP_BLOCK
    row = lax.broadcasted_iota(jnp.int32, (nb, RET_DH), 0)
    zpad = jnp.zeros((RET_CHUNK - nb, RET_DH), F32)
    for hh in range(RET_HEADS):
        l0 = hh * RET_DH
        gam = _ret_gamma(hh)
        qb = q_ref[:, l0:l0 + RET_DH].astype(BF16)
        kb = k_ref[:, l0:l0 + RET_DH].astype(BF16)
        vb = v_ref[:, l0:l0 + RET_DH].astype(BF16)
        s_list = [s0_ref[n, hh] for n in range(nb)]
        scat = jnp.concatenate(s_list, axis=1).astype(BF16)
        res = jnp.dot(qb, scat, preferred_element_type=F32)
        cross = jnp.zeros((nb, RET_DH), F32)
        for n in range(nb):
            cross = cross + jnp.where(row == n, res[:, n * RET_DH:(n + 1) * RET_DH], 0.0)
        sc = jnp.sum(qb.astype(F32) * kb.astype(F32), axis=-1, keepdims=True)
        o_ref[:, l0:l0 + RET_DH] = sc.astype(BF16).astype(F32) * vb.astype(F32) + cross * gam
        kpad = jnp.concatenate([kb.astype(F32), zpad], axis=0).astype(BF16)
        vf = vb.astype(F32)
        w = jnp.concatenate([jnp.where(row == n, vf, 0.0) for n in range(nb)], axis=1)
        wpad = jnp.concatenate([w, jnp.zeros((RET_CHUNK - nb, nb * RET_DH), F32)], axis=0).astype(BF16)
        kv = lax.dot_general(kpad, wpad, TN, preferred_element_type=F32)
        for n in range(nb):
            snew_ref[n, hh] = gam * s_list[n] + kv[:, n * RET_DH:(n + 1) * RET_DH]


def _smp_post_kernel(x_ref, mod_ref, o_ref, g_ref, cout_ref, wout_ref, gpost1_ref, gpre2_ref,
                     x1_ref, h2_ref):
    d = x_ref.shape[1]
    rw = RET_HEADS * RET_DH
    x = x_ref[...]
    gt1 = mod_ref[:, 2 * d:3 * d]
    sh2 = mod_ref[:, 3 * d:4 * d]
    sc2 = mod_ref[:, 4 * d:5 * d]
    m = jnp.dot(cout_ref[...].astype(BF16), wout_ref[rw:, :], preferred_element_type=F32)
    for hh in range(RET_HEADS):
        l0 = hh * RET_DH
        o = o_ref[:, l0:l0 + RET_DH]
        oc = o - jnp.mean(o, axis=-1, keepdims=True)
        on = oc * lax.rsqrt(jnp.mean(oc * oc, axis=-1, keepdims=True) + EPS)
        r = (_silu(g_ref[:, l0:l0 + RET_DH]) * on).astype(BF16)
        m = m + jnp.dot(r, wout_ref[l0:l0 + RET_DH, :], preferred_element_type=F32)
    x1 = x + gt1 * _rms(m, gpost1_ref[...])
    x1_ref[...] = x1
    h2_ref[...] = (_rms(x1, gpre2_ref[...]) * (1.0 + sc2) + sh2).astype(BF16)


def _sample_mixer(x, mod, state, cache, pos, w_in, w_out, g_pre1, g_post1, g_pre2, conv_w, conv_b):
    n, d = x.shape
    rw = RET_HEADS * RET_DH
    cwid = conv_w.shape[1]
    cos, sin = _rope_tables(pos)
    f = jax.ShapeDtypeStruct
    q, k, v, g, cout, cache_out = pl.pallas_call(
        _smp_proj_kernel,
        out_shape=[f((n, rw), F32)] * 4 + [f((n, cwid), F32), f((n, (CONV_K - 1) * cwid), F32)],
        compiler_params=pltpu.CompilerParams(vmem_limit_bytes=VMEM_LIMIT),
        name="smp_proj",
    )(x, mod, cos, sin, w_in, g_pre1.reshape(1, d), conv_w, conv_b.reshape(1, cwid),
      cache.reshape(n, (CONV_K - 1) * cwid))
    nb = SMP_BLOCK
    vec = pl.BlockSpec((nb, rw), lambda i: (i, 0))
    st = pl.BlockSpec((nb, RET_HEADS, RET_DH, RET_DH), lambda i: (i, 0, 0, 0))
    o, snew = pl.pallas_call(
        _smp_state_kernel,
        grid=(n // nb,),
        in_specs=[vec, vec, vec, st],
        out_specs=[vec, st],
        out_shape=[f((n, rw), F32), f(state.shape, F32)],
        compiler_params=_cparams(("parallel",)),
        name="smp_state",
    )(q, k, v, state)
    x1, h2 = pl.pallas_call(
        _smp_post_kernel,
        out_shape=[f((n, d), F32), f((n, d), BF16)],
        compiler_params=pltpu.CompilerParams(vmem_limit_bytes=VMEM_LIMIT),
        name="smp_post",
    )(x, mod, o, g, cout, w_out, g_post1.reshape(1, d), g_pre2.reshape(1, d))
    return x1, h2, snew, cache_out.reshape(n, CONV_K - 1, cwid)


_CAND = [(a, b) for a in range(PEER_TOPK) for b in range(PEER_TOPK) if (a + 1) * (b + 1) <= PEER_TOPK]


def _oddeven_mergesort_network(n):
    comps = []
    p = 1
    while p < n:
        k = p
        while k >= 1:
            for j in range(k % p, n - k, 2 * k):
                for i in range(min(k, n - j - k)):
                    if (i + j) // (2 * p) == (i + j + k) // (2 * p):
                        comps.append((i + j, i + j + k))
            k //= 2
        p *= 2
    return comps


_SORT16 = _oddeven_mergesort_network(PEER_TOPK)


def _sort16(x):
    x = list(x)
    for i, j in _SORT16:
        x[i], x[j] = jnp.maximum(x[i], x[j]), jnp.minimum(x[i], x[j])
    return x


def _merge_top16(x, y):
    x = [jnp.maximum(x[k], y[PEER_TOPK - 1 - k]) for k in range(PEER_TOPK)]
    d = PEER_TOPK // 2
    while d >= 1:
        for i in range(PEER_TOPK):
            if i & d == 0:
                x[i], x[i + d] = jnp.maximum(x[i], x[i + d]), jnp.minimum(x[i], x[i + d])
        d //= 2
    return x


def _sorted_top16(x):
    x = _sort16(x)
    for shift in (4, 2, 1):
        x = _merge_top16(x, [pltpu.roll(v, shift, axis=0) for v in x])
    return x


def _sel_kernel(h2_ref, wq_ref, keys_ref, e1_ref, cnt_ref, e2_ref, r2_ref,
                s1_scr, s2_scr, a_scr, b_scr, *, tt):
    ntc = tt // LANES
    neg = -jnp.inf
    q = jnp.dot(h2_ref[...], wq_ref[...], preferred_element_type=F32).astype(BF16)
    for h in range(PEER_HEADS):
        for c, scr in ((0, s1_scr), (1, s2_scr)):
            col = (h * 2 + c) * PEER_HALF
            scr[h] = lax.dot_general(keys_ref[h, c], q[:, col:col + PEER_HALF], NT,
                                     preferred_element_type=F32)

    def token_chunk(tc, carry):
        lanes = pl.ds(pl.multiple_of(tc * LANES, LANES), LANES)
        for h in range(PEER_HEADS):
            for side, (s_scr, v_scr) in enumerate(((s1_scr, a_scr), (s2_scr, b_scr))):
                s = s_scr[h, :, lanes]
                top = _sorted_top16([s[v * 8:(v + 1) * 8] for v in range(PEER_NKEYS // 8)])
                for k in range(PEER_TOPK):
                    v_scr[k, h:h + 1, lanes] = top[k][0:1]
                if side == 1:
                    rank = jnp.zeros(s.shape, F32)
                    for k in range(PEER_TOPK):
                        rank = jnp.where(top[k][0:1] > s, float(k + 1), rank)
                    r2_ref[h, :, lanes] = rank.astype(BF16)
        av = [a_scr[k, :, lanes] for k in range(PEER_TOPK)]
        bv = [b_scr[k, :, lanes] for k in range(PEER_TOPK)]
        cands = [av[a] + bv[b] for a, b in _CAND]
        pad = [jnp.full(cands[0].shape, neg, F32)] * (-len(cands) % PEER_TOPK)
        groups = [_sort16((cands + pad)[g:g + PEER_TOPK]) for g in range(0, len(cands) + len(pad), PEER_TOPK)]
        while len(groups) > 1:
            groups = [_merge_top16(groups[g], groups[g + 1]) for g in range(0, len(groups), 2)]
        t = groups[0][PEER_TOPK - 1]
        z = None
        for c in cands:
            term = jnp.where(c >= t, jnp.exp(c - cands[0]), 0.0)
            z = term if z is None else z + term
        zi = 1.0 / z
        cntr = []
        for a in range(PEER_TOPK):
            c = jnp.zeros(t.shape, F32)
            for b in range(PEER_TOPK):
                c = jnp.where(av[a] + bv[b] >= t, float(b + 1), c)
            cntr.append(c)
        for h in range(PEER_HEADS):
            s1 = s1_scr[h, :, lanes]
            s2 = s2_scr[h, :, lanes]
            cnt = jnp.zeros(s1.shape, F32)
            for k in reversed(range(PEER_TOPK)):
                cnt = jnp.where(s1 >= av[k][h:h + 1], cntr[k][h:h + 1], cnt)
            cnt_ref[h, :, lanes] = cnt
            e1_ref[h, :, lanes] = jnp.exp(s1 - av[0][h:h + 1]) * zi[h:h + 1]
            e2_ref[h, :, lanes] = jnp.exp(s2 - bv[0][h:h + 1]).astype(BF16)
        return carry

    lax.fori_loop(0, ntc, token_chunk, 0)


def _peer_select(h2, w_q, keys, tt):
    t, d = h2.shape
    shape = (PEER_HEADS, PEER_NKEYS, t)
    out = [jax.ShapeDtypeStruct(shape, F32)] * 2 + [jax.ShapeDtypeStruct(shape, BF16)] * 2
    ospec = pl.BlockSpec((PEER_HEADS, PEER_NKEYS, tt), lambda i: (0, 0, i))
    return pl.pallas_call(
        functools.partial(_sel_kernel, tt=tt),
        grid=(t // tt,),
        in_specs=[pl.BlockSpec((tt, d), lambda i: (i, 0)),
                  pl.BlockSpec(w_q.shape, lambda i: (0, 0)),
                  pl.BlockSpec(keys.shape, lambda i: (0, 0, 0, 0))],
        out_specs=[ospec] * 4,
        out_shape=out,
        scratch_shapes=[pltpu.VMEM((PEER_HEADS, PEER_NKEYS, tt), F32),
                        pltpu.VMEM((PEER_HEADS, PEER_NKEYS, tt), F32),
                        pltpu.VMEM((PEER_TOPK, PEER_HEADS, tt), F32),
                        pltpu.VMEM((PEER_TOPK, PEER_HEADS, tt), F32)],
        compiler_params=_cparams(("parallel",)),
        name="sel",
    )(h2, w_q, keys)


PEER_ECHUNK = 512
PEER_JSUB = 16
PEER_IGROUP = 8
PEER_GSPAN = 2
PEER_LANE_SPLIT = 256


def _gelu_tanh_bf16(x):
    c = np.sqrt(2.0 / np.pi)
    c0 = jnp.full(x.shape, c, F32).astype(x.dtype)
    c1 = jnp.full(x.shape, 0.044715 * c, F32).astype(x.dtype)
    t = jnp.tanh(x * (c0 + c1 * (x * x)))
    return x * (0.5 + 0.5 * t)


def _peer_kernel(h2_ref, u_ref, vt_ref, e1_ref, cnt_ref, e2_ref, r2_ref, x1_ref, gt2_ref, gpost2_ref,
                 y_ref, acc_ref, c_scr, er_scr, xt_scr, *, eb):
    e = pl.program_id(1)
    tt = h2_ref.shape[0]
    ipc = PEER_ECHUNK // PEER_NKEYS
    r2_off = PEER_HEADS * PEER_NKEYS + PEER_JSUB

    @pl.when(e == 0)
    def _():
        acc_ref[...] = jnp.zeros_like(acc_ref)
        xt_scr[...] = h2_ref[...].T
        for h in range(PEER_HEADS):
            er_scr[h * PEER_NKEYS:(h + 1) * PEER_NKEYS, 0:tt] = e2_ref[h].astype(F32)
            er_scr[r2_off + h * PEER_NKEYS:r2_off + (h + 1) * PEER_NKEYS, 0:tt] = r2_ref[h].astype(F32)

    bshape = (PEER_JSUB, LANES)
    span = PEER_GSPAN * PEER_ECHUNK
    hw = min(tt, PEER_LANE_SPLIT)
    for sp, hf in [(sp, hf) for hf in range(tt // hw) for sp in range(eb // span)]:
        s0 = sp * span
        a = [jnp.dot(u_ref[s0 + q * PEER_ECHUNK:s0 + (q + 1) * PEER_ECHUNK, :], xt_scr[:, hf * hw:(hf + 1) * hw],
                     preferred_element_type=F32) for q in range(PEER_GSPAN)]
        for lc in range(hw // LANES):
            lanes = slice(hf * hw + lc * LANES, hf * hw + (lc + 1) * LANES)
            la = slice(lc * LANES, (lc + 1) * LANES)
            for ig in range(PEER_GSPAN * ipc // PEER_IGROUP):
                ils = [ig * PEER_IGROUP + t for t in range(PEER_IGROUP)]
                e1b, cntb = [], []
                for h in range(PEER_HEADS):
                    rows = [sp * PEER_GSPAN * ipc + il for il in ils]
                    e1b.append([jnp.broadcast_to(e1_ref[h, i:i + 1, lanes], bshape).astype(BF16) for i in rows])
                    cntb.append([jnp.broadcast_to(cnt_ref[h, i:i + 1, lanes], bshape).astype(BF16) for i in rows])
                for jc in range(PEER_NKEYS // PEER_JSUB):
                    g = [None] * PEER_IGROUP
                    for h in range(PEER_HEADS):
                        j0 = h * PEER_NKEYS + jc * PEER_JSUB
                        r2v = er_scr[r2_off + j0:r2_off + j0 + PEER_JSUB, lanes].astype(BF16)
                        e2v = er_scr[j0:j0 + PEER_JSUB, lanes].astype(BF16)
                        for t in range(PEER_IGROUP):
                            term = jnp.where(r2v < cntb[h][t], e2v, 0.0) * e1b[h][t]
                            g[t] = term if g[t] is None else g[t] + term
                    for t, il in enumerate(ils):
                        r0 = il * PEER_NKEYS + jc * PEER_JSUB
                        aq = a[r0 // PEER_ECHUNK]
                        ra = r0 % PEER_ECHUNK
                        act = _gelu_tanh_bf16(aq[ra:ra + PEER_JSUB, la].astype(BF16))
                        c_scr[s0 + r0:s0 + r0 + PEER_JSUB, lanes] = act * g[t]
    for hf in range(tt // hw):
        hl = slice(hf * hw, (hf + 1) * hw)
        acc_ref[:, hl] += jnp.dot(vt_ref[...], c_scr[:, hl], preferred_element_type=F32)

    @pl.when(e == pl.num_programs(1) - 1)
    def _():
        f = acc_ref[...].T
        gt2 = gt2_ref[0] if len(gt2_ref.shape) == 3 else gt2_ref[...]
        y_ref[...] = x1_ref[...] + gt2 * _rms(f, gpost2_ref[...])


def _peer_dense(h2, x1, gt2, ub, vt, sel, g_post2, tt, eb, seq_len):
    t, d = h2.shape
    ne = ub.shape[0]
    e1, cnt, e2, r2 = sel
    ib = eb // PEER_NKEYS
    per_e = pl.BlockSpec((PEER_HEADS, ib, tt), lambda i, e: (0, e, i))
    per_t = pl.BlockSpec((PEER_HEADS, PEER_NKEYS, tt), lambda i, e: (0, 0, i))
    if seq_len == 1:
        gspec = pl.BlockSpec((tt, d), lambda i, e: (i, 0))
    else:
        assert seq_len % tt == 0
        tiles_per_seq = seq_len // tt
        gspec = pl.BlockSpec((1, 1, d), lambda i, e: (i // tiles_per_seq, 0, 0))
        gt2 = gt2.reshape(gt2.shape[0], 1, d)
    return pl.pallas_call(
        functools.partial(_peer_kernel, eb=eb),
        grid=(t // tt, ne // eb),
        in_specs=[pl.BlockSpec((tt, d), lambda i, e: (i, 0)),
                  pl.BlockSpec((eb, d), lambda i, e: (e, 0)),
                  pl.BlockSpec((d, eb), lambda i, e: (0, e)),
                  per_e, per_e, per_t, per_t,
                  pl.BlockSpec((tt, d), lambda i, e: (i, 0)),
                  gspec,
                  pl.BlockSpec((1, d), lambda i, e: (0, 0))],
        out_specs=pl.BlockSpec((tt, d), lambda i, e: (i, 0)),
        out_shape=jax.ShapeDtypeStruct((t, d), F32),
        scratch_shapes=[pltpu.VMEM((d, tt), F32), pltpu.VMEM((eb, tt), BF16),
                        pltpu.VMEM((2 * PEER_HEADS * PEER_NKEYS + PEER_JSUB, tt + LANES), F32),
                        pltpu.VMEM((d, tt), BF16)],
        compiler_params=_cparams(("parallel", "arbitrary")),
        name="peer",
    )(h2, ub, vt, e1, cnt, e2, r2, x1, gt2, g_post2.reshape(1, d))


def _prep_kernel(u_ref, v_ref, ub_ref, vt_ref):
    ub_ref[...] = u_ref[...].astype(BF16)
    vt_ref[...] = v_ref[...].T.astype(BF16)


def _prep_tables(u_tab, v_tab, blk=512):
    ne, d = u_tab.shape
    return pl.pallas_call(
        _prep_kernel,
        grid=(ne // blk,),
        in_specs=[pl.BlockSpec((blk, d), lambda i: (i, 0)), pl.BlockSpec((blk, d), lambda i: (i, 0))],
        out_specs=[pl.BlockSpec((blk, d), lambda i: (i, 0)), pl.BlockSpec((d, blk), lambda i: (0, i))],
        out_shape=[jax.ShapeDtypeStruct((ne, d), BF16), jax.ShapeDtypeStruct((d, ne), BF16)],
        compiler_params=_cparams(("parallel",)),
        name="prep",
    )(u_tab, v_tab)


PAST_LEN = 16384
MIX_TILE = 512
SEL_TILE = 512
PEER_TILE = 512
PEER_EBLOCK = 2048


def kernel(x_prompt, x_sample, c_prompt, c_sample, state_ret, cache_conv, w_ada, b_ada, g_pre1, g_post1, g_pre2, g_post2, w_in, w_out, conv_w, conv_b, w_q, sub_keys, u_tab, v_tab):
    depth = w_in.shape[0]
    assert depth == 1, "single-layer stack"
    l = 0
    b, s, d = x_prompt.shape
    n = x_sample.shape[0]
    assert x_sample.shape[1] == 1
    mod = _modulation(jnp.concatenate([c_prompt, c_sample], axis=0), w_ada[l], b_ada[l])
    mod_p, mod_s = mod[:b], mod[b:]
    w_in_b, w_out_b = w_in[l].astype(BF16), w_out[l].astype(BF16)
    w_q_b, keys_b = w_q[l].astype(BF16), sub_keys[l].astype(BF16)
    ub, vt = _prep_tables(u_tab[l], v_tab[l])

    x1p, h2p, sfin_p, cache_p = _prompt_mixer(
        x_prompt, mod_p.reshape(b, N_MOD, d), w_in_b, w_out_b,
        g_pre1[l], g_post1[l], g_pre2[l], conv_w[l], conv_b[l], min(MIX_TILE, s))
    x1s, h2s, sfin_s, cache_s = _sample_mixer(
        x_sample.reshape(n, d), mod_s, state_ret[l], cache_conv[l],
        PAST_LEN + jnp.arange(1, dtype=jnp.int32), w_in_b, w_out_b,
        g_pre1[l], g_post1[l], g_pre2[l], conv_w[l], conv_b[l])

    h2p = h2p.reshape(b * s, d)
    sel_p = _peer_select(h2p, w_q_b, keys_b, min(SEL_TILE, s))
    yp = _peer_dense(h2p, x1p.reshape(b * s, d), mod_p[:, 5 * d:6 * d], ub, vt, sel_p, g_post2[l],
                     min(PEER_TILE, s), PEER_EBLOCK, s)
    sel_s = _peer_select(h2s, w_q_b, keys_b, n)
    ys = _peer_dense(h2s, x1s, mod_s[:, 5 * d:6 * d], ub, vt, sel_s, g_post2[l], n, PEER_EBLOCK, 1)

    return (yp.reshape(b, s, d), ys.reshape(n, 1, d), sfin_p[None], cache_p[None],
            sfin_s[None], cache_s[None])
```
